```python
import jax, jax.numpy as jnp
from jax import lax
import numpy as np

D_MODEL = 1024
BATCH = 8
SEQ = 2048
DEPTH = 2

CHUNK = 64
N_META = 16

SB_HEAD_DIM = 64
SB_WIDTH = D_MODEL // 2
N_SB_HEADS = SB_WIDTH // SB_HEAD_DIM
Q_BLOCK = 128

POOL_WINDOWS = (2, 4, 8, 16)
N_POOL_GROUPS = len(POOL_WINDOWS)
POOL_WIDTH = D_MODEL // 2
POOL_GROUP_DIM = POOL_WIDTH // N_POOL_GROUPS

N_BRANCHES = 2
N_IN = 3 * SB_WIDTH + POOL_WIDTH + N_BRANCHES * D_MODEL

D_FF = ((-(-8 * D_MODEL // 3) + 255) // 256) * 256
RMS_EPS = 1e-6

kernel_name = "hybrid_pool_stickbreak_meta"


def rms_norm(x, g):
    xf = x.astype(jnp.float32)
    y = xf * lax.rsqrt(jnp.mean(xf * xf, axis=-1, keepdims=True) + RMS_EPS)
    return (y * g.astype(jnp.float32)).astype(x.dtype)


def pool_mixer(u, mix, scale):
    b, l, _ = u.shape
    uf = u.astype(jnp.float32)
    cs0 = jnp.pad(jnp.cumsum(uf, axis=1), ((0, 0), (1, 0), (0, 0)))
    pos = jnp.arange(l)
    means = []
    for g, w in enumerate(POOL_WINDOWS):
        c = cs0[:, :, g * POOL_GROUP_DIM:(g + 1) * POOL_GROUP_DIM]
        cur = c[:, 1:]
        lag = jnp.pad(c, ((0, 0), (w - 1, 0), (0, 0)))[:, :l]
        cnt = jnp.minimum(pos + 1, w).astype(jnp.float32)[None, :, None]
        means.append((cur - lag) / cnt)
    pooled = jnp.stack(means, axis=2)
    diff = (pooled - uf.reshape(b, l, N_POOL_GROUPS, POOL_GROUP_DIM)).astype(u.dtype)
    mixed = jnp.einsum('blgc,gcd->blgd', diff, mix)
    return mixed.reshape(b, l, POOL_WIDTH) * scale


def stick_breaking_attention(q, k, v):
    b, l, h, dh = q.shape
    lp = -(-l // Q_BLOCK) * Q_BLOCK
    padw = ((0, 0), (0, lp - l), (0, 0), (0, 0))
    qh, kh, vh = [jnp.pad(a, padw).transpose(0, 2, 1, 3) for a in (q, k, v)]
    scale = 1.0 / float(np.sqrt(dh))
    outs = []
    for i in range(lp // Q_BLOCK):
        q0 = i * Q_BLOCK
        kend = q0 + Q_BLOCK
        qb = qh[:, :, q0:kend]
        kb = kh[:, :, :kend]
        vb = vh[:, :, :kend]
        z = jnp.einsum('bhqd,bhkd->bhqk', qb, kb).astype(jnp.float32) * scale
        t_idx = q0 + jnp.arange(Q_BLOCK)
        s_idx = jnp.arange(kend)
        mask = s_idx[None, :] < t_idx[:, None]
        log_keep = jnp.where(mask, jax.nn.log_sigmoid(-z), 0.0)
        log_stick = lax.cumsum(log_keep, axis=3, reverse=True) - log_keep
        w = jnp.where(mask, jnp.exp(jax.nn.log_sigmoid(z) + log_stick), 0.0)
        outs.append(jnp.einsum('bhqk,bhkd->bhqd', w.astype(v.dtype), vb))
    o = jnp.concatenate(outs, axis=2)[:, :, :l]
    return o.transpose(0, 2, 1, 3).reshape(b, l, h * dh)


def hybrid_layer(x, norm1_g, w_in, b_gate, pool_mix, pool_scale, w_branch_pool, w_branch_sb,
                 w_out, norm2_g, w_ffn_in, w_ffn_out):
    b, l, _ = x.shape
    h = rms_norm(x, norm1_g)
    proj = h @ w_in
    q, k, v, u, gate_pre = jnp.split(
        proj, [SB_WIDTH, 2 * SB_WIDTH, 3 * SB_WIDTH, 3 * SB_WIDTH + POOL_WIDTH], axis=-1)
    gates = jax.nn.sigmoid(gate_pre + b_gate)
    g_pool, g_sb = jnp.split(gates, N_BRANCHES, axis=-1)
    a = pool_mixer(u, pool_mix, pool_scale)
    hs = (b, l, N_SB_HEADS, SB_HEAD_DIM)
    s = stick_breaking_attention(q.reshape(hs), k.reshape(hs), v.reshape(hs))
    merged = g_pool * (a @ w_branch_pool) + g_sb * (s @ w_branch_sb)
    x = x + merged @ w_out
    h2 = rms_norm(x, norm2_g)
    gt, up = jnp.split(h2 @ w_ffn_in, 2, axis=-1)
    return x + (jax.nn.silu(gt) * up) @ w_ffn_out


def setup_inputs(seed: int = 0) -> dict:
    key = jax.random.key(seed)
    ks = jax.random.split(key, 16)
    f = jnp.float32
    nrm = lambda kk, shape, s: jax.random.normal(kk, shape, f) * s
    return {
        "x": nrm(ks[0], (BATCH, SEQ, D_MODEL), 1.0),
        "meta_tokens": nrm(ks[1], (N_META, D_MODEL), 1.0),
        "norm1_g": 1.0 + nrm(ks[2], (DEPTH, D_MODEL), 0.02),
        "w_in": nrm(ks[3], (DEPTH, D_MODEL, N_IN), D_MODEL ** -0.5),
        "b_gate": nrm(ks[4], (DEPTH, N_BRANCHES * D_MODEL), 0.02),
        "pool_mix": nrm(ks[5], (DEPTH, N_POOL_GROUPS, POOL_GROUP_DIM, POOL_GROUP_DIM), POOL_GROUP_DIM ** -0.5),
        "pool_scale": 1.0 + nrm(ks[6], (DEPTH, POOL_WIDTH), 0.02),
        "w_branch_pool": nrm(ks[7], (DEPTH, POOL_WIDTH, D_MODEL), POOL_WIDTH ** -0.5),
        "w_branch_sb": nrm(ks[8], (DEPTH, SB_WIDTH, D_MODEL), SB_WIDTH ** -0.5),
        "w_out": nrm(ks[9], (DEPTH, D_MODEL, D_MODEL), D_MODEL ** -0.5),
        "norm2_g": 1.0 + nrm(ks[10], (DEPTH, D_MODEL), 0.02),
        "w_ffn_in": nrm(ks[11], (DEPTH, D_MODEL, 2 * D_FF), D_MODEL ** -0.5),
        "w_ffn_out": nrm(ks[12], (DEPTH, D_FF, D_MODEL), D_FF ** -0.5),
        "final_norm_g": 1.0 + nrm(ks[13], (D_MODEL,), 0.02),
    }


def reference(x, meta_tokens, norm1_g, w_in, b_gate, pool_mix, pool_scale, w_branch_pool,
              w_branch_sb, w_out, norm2_g, w_ffn_in, w_ffn_out, final_norm_g):
    b = x.shape[0]
    meta = jnp.broadcast_to(meta_tokens[None].astype(x.dtype), (b, N_META, x.shape[-1]))
    hcat = jnp.concatenate([meta, x], axis=1)
    for layer in range(DEPTH):
        hcat = hybrid_layer(hcat, norm1_g[layer], w_in[layer], b_gate[layer], pool_mix[layer],
                            pool_scale[layer], w_branch_pool[layer], w_branch_sb[layer],
                            w_out[layer], norm2_g[layer], w_ffn_in[layer], w_ffn_out[layer])
    hcat = rms_norm(hcat, final_norm_g)
    return hcat[:, N_META:]
```

```python
import functools

import jax
import jax.numpy as jnp
from jax import lax
from jax.experimental import pallas as pl
from jax.experimental.pallas import tpu as pltpu

D_MODEL = 1024
N_META = 16
SB_HEAD_DIM = 64
SB_WIDTH = 512
POOL_WINDOWS = (2, 4, 8, 16)
POOL_WIDTH = 512
POOL_GROUP_DIM = 128
N_IN = 4096
D_FF = 2816
RMS_EPS = 1e-6

LANES = 128
Q_BLOCK = 128
K_BLOCK = 256
HALO = 16
FF_CHUNK = 256
COL_CHUNK = 512
VMEM_LIMIT_BYTES = 56 * 1024 * 1024

BF16 = jnp.bfloat16
F32 = jnp.float32


def _dot(a, b):
    return jnp.dot(a, b, preferred_element_type=F32)


def _rms_norm_rows(x, g):
    ms = jnp.mean(x * x, axis=-1, keepdims=True)
    return x * lax.rsqrt(ms + RMS_EPS) * g


def _sigmoid(x):
    return 1.0 / (1.0 + jnp.exp(-x))


def _input_stage_kernel(tiles_per_seq, x_ref, g_ref, w_ref, bg_ref, mix_ref, ps_ref, wbp_ref,
                        q_ref, k_ref, v_ref, p_ref, gsb_ref,
                        h_ref, ubuf_ref, a_ref):
    tm = x_ref.shape[0]
    tile_in_seq = pl.program_id(0) % tiles_per_seq

    h_ref[...] = _rms_norm_rows(x_ref[...], g_ref[...]).astype(BF16)

    @pl.when(tile_in_seq == 0)
    def _():
        ubuf_ref[0:HALO, :] = jnp.zeros((HALO, POOL_WIDTH), F32)

    @pl.when(tile_in_seq != 0)
    def _():
        ubuf_ref[0:HALO, :] = ubuf_ref[tm:tm + HALO, :]

    h = h_ref[...]
    q_ref[...] = (_dot(h, w_ref[:, 0:SB_WIDTH]) * (SB_HEAD_DIM ** -0.5)).astype(BF16)
    k_ref[...] = _dot(h, w_ref[:, SB_WIDTH:2 * SB_WIDTH]).astype(BF16)
    v_ref[...] = _dot(h, w_ref[:, 2 * SB_WIDTH:3 * SB_WIDTH]).astype(BF16)
    ubuf_ref[HALO:HALO + tm, :] = _dot(h, w_ref[:, 3 * SB_WIDTH:3 * SB_WIDTH + POOL_WIDTH])

    pos = tile_in_seq * tm + lax.broadcasted_iota(jnp.int32, (tm, 1), 0)
    for g, window in enumerate(POOL_WINDOWS):
        cols = slice(g * POOL_GROUP_DIM, (g + 1) * POOL_GROUP_DIM)
        ug = ubuf_ref[HALO:HALO + tm, cols]
        total = ug
        for d in range(1, window):
            total = total + ubuf_ref[HALO - d:HALO - d + tm, cols]
        cnt = jnp.minimum(pos + 1, window).astype(F32)
        diff = (total / cnt - ug).astype(BF16)
        a_ref[:, cols] = (_dot(diff, mix_ref[g]) * ps_ref[:, cols]).astype(BF16)

    gate0 = 3 * SB_WIDTH + POOL_WIDTH
    a = a_ref[...]
    for c in range(D_MODEL // COL_CHUNK):
        cols = slice(c * COL_CHUNK, (c + 1) * COL_CHUNK)
        g_pool = _sigmoid(_dot(h, w_ref[:, gate0 + c * COL_CHUNK:gate0 + (c + 1) * COL_CHUNK])
                          + bg_ref[:, cols])
        p_ref[:, cols] = g_pool * _dot(a, wbp_ref[:, cols])
        s0 = gate0 + D_MODEL + c * COL_CHUNK
        gsb_ref[:, cols] = _sigmoid(_dot(h, w_ref[:, s0:s0 + COL_CHUNK])
                                    + bg_ref[:, D_MODEL + c * COL_CHUNK:D_MODEL + (c + 1) * COL_CHUNK])


def _input_stage(x, g, w, bg, mix, ps, wbp, *, tm, tiles_per_seq):
    t = x.shape[0]
    full = lambda shape: pl.BlockSpec(shape, lambda i: (0,) * len(shape))
    rows = lambda width: pl.BlockSpec((tm, width), lambda i: (i, 0))
    return pl.pallas_call(
        functools.partial(_input_stage_kernel, tiles_per_seq),
        grid=(t // tm,),
        in_specs=[rows(D_MODEL), full((1, D_MODEL)), full((D_MODEL, N_IN)), full((1, 2 * D_MODEL)),
                  full((len(POOL_WINDOWS), POOL_GROUP_DIM, POOL_GROUP_DIM)), full((1, POOL_WIDTH)),
                  full((POOL_WIDTH, D_MODEL))],
        out_specs=[rows(SB_WIDTH), rows(SB_WIDTH), rows(SB_WIDTH), rows(D_MODEL), rows(D_MODEL)],
        out_shape=[jax.ShapeDtypeStruct((t, SB_WIDTH), BF16)] * 3
                  + [jax.ShapeDtypeStruct((t, D_MODEL), F32)] * 2,
        scratch_shapes=[pltpu.VMEM((tm, D_MODEL), BF16),
                        pltpu.VMEM((tm + HALO, POOL_WIDTH), F32),
                        pltpu.VMEM((tm, POOL_WIDTH), BF16)],
        compiler_params=pltpu.CompilerParams(dimension_semantics=("arbitrary",),
                                             vmem_limit_bytes=VMEM_LIMIT_BYTES),
        name="input_stage",
    )(x, g, w, bg, mix, ps, wbp)


def _attention_kernel(q_ref, k_ref, v_ref, tri_k_ref, tri_q_ref, o_ref, acc_ref, carry_ref):
    qi = pl.program_id(2)
    q2 = q_ref[0]
    lane = lax.broadcasted_iota(jnp.int32, q2.shape, 1)
    zero = jnp.zeros_like(q2)
    qs = jnp.concatenate([jnp.where(lane < SB_HEAD_DIM, q2, zero),
                          jnp.where(lane >= SB_HEAD_DIM, q2, zero)], axis=0)

    acc_ref[...] = jnp.zeros_like(acc_ref)
    carry_ref[...] = jnp.zeros_like(carry_ref)

    def visit(start, width, tri_ref, diagonal):
        kb = k_ref[0, pl.ds(start, width), :]
        vb = v_ref[0, pl.ds(start, width), :]
        z = lax.dot_general(qs, kb, (((1,), (1,)), ((), ())), preferred_element_type=F32)
        log_keep = -(jnp.maximum(z, 0.0) + jnp.log(1.0 + jnp.exp(-jnp.abs(z))))
        if diagonal:
            row = lax.broadcasted_iota(jnp.int32, z.shape, 0) & (Q_BLOCK - 1)
            col = lax.broadcasted_iota(jnp.int32, z.shape, 1)
            mask = col < row
            log_keep = jnp.where(mask, log_keep, 0.0)
        hi = log_keep.astype(BF16)
        lo = (log_keep - hi.astype(F32)).astype(BF16)
        stick = _dot(jnp.concatenate([hi, lo], axis=1), tri_ref[...])
        w = jnp.exp(z + log_keep + stick)
        if diagonal:
            w = jnp.where(mask, w, 0.0)
        pv = _dot(w.astype(BF16), vb)
        carry = carry_ref[...]
        acc_ref[...] += jnp.exp(carry) * pv
        carry_ref[...] = carry + stick[:, 0:1] + log_keep[:, 0:1]

    visit(pl.multiple_of(qi * Q_BLOCK, Q_BLOCK), Q_BLOCK, tri_q_ref, True)

    @pl.when(qi % 2 == 1)
    def _():
        visit(pl.multiple_of((qi - 1) * Q_BLOCK, Q_BLOCK), Q_BLOCK, tri_q_ref, False)

    n_full = qi // 2

    def body(j, c):
        visit(pl.multiple_of((n_full - 1 - j) * K_BLOCK, K_BLOCK), K_BLOCK, tri_k_ref, False)
        return c

    lax.fori_loop(0, n_full, body, 0)

    acc = acc_ref[...]
    o_ref[0] = jnp.where(lane < SB_HEAD_DIM, acc[:Q_BLOCK], acc[Q_BLOCK:]).astype(o_ref.dtype)


def _suffix_matrix(n):
    j = lax.broadcasted_iota(jnp.int32, (n, n), 0)
    s = lax.broadcasted_iota(jnp.int32, (n, n), 1)
    t = (j > s).astype(BF16)
    return jnp.concatenate([t, t], axis=0)


def _attention(q, k, v):
    b, lp, _ = q.shape
    n_pairs = SB_WIDTH // LANES
    tri_k = _suffix_matrix(K_BLOCK)
    tri_q = _suffix_matrix(Q_BLOCK)
    return pl.pallas_call(
        _attention_kernel,
        grid=(b, n_pairs, lp // Q_BLOCK),
        in_specs=[pl.BlockSpec((1, Q_BLOCK, LANES), lambda bi, hp, qi: (bi, qi, hp)),
                  pl.BlockSpec((1, lp, LANES), lambda bi, hp, qi: (bi, 0, hp)),
                  pl.BlockSpec((1, lp, LANES), lambda bi, hp, qi: (bi, 0, hp)),
                  pl.BlockSpec((2 * K_BLOCK, K_BLOCK), lambda bi, hp, qi: (0, 0)),
                  pl.BlockSpec((2 * Q_BLOCK, Q_BLOCK), lambda bi, hp, qi: (0, 0))],
        out_specs=pl.BlockSpec((1, Q_BLOCK, LANES), lambda bi, hp, qi: (bi, qi, hp)),
        out_shape=jax.ShapeDtypeStruct((b, lp, SB_WIDTH), BF16),
        scratch_shapes=[pltpu.VMEM((2 * Q_BLOCK, LANES), F32),
                        pltpu.VMEM((2 * Q_BLOCK, 1), F32)],
        compiler_params=pltpu.CompilerParams(
            dimension_semantics=("arbitrary", "arbitrary", "arbitrary")),
        name="stick_breaking_attention",
    )(q, k, v, tri_k, tri_q)


def _output_stage_kernel(final_norm, x_ref, s_ref, p_ref, gsb_ref, wbs_ref, wo_ref, g2_ref,
                         wfi_ref, wfo_ref, gf_ref, o_ref, m_ref, x1_ref, h2_ref, act_ref):
    s = s_ref[...]
    for c in range(D_MODEL // COL_CHUNK):
        cols = slice(c * COL_CHUNK, (c + 1) * COL_CHUNK)
        m_ref[:, cols] = (p_ref[:, cols] + gsb_ref[:, cols] * _dot(s, wbs_ref[:, cols])).astype(BF16)
    m = m_ref[...]
    for c in range(D_MODEL // COL_CHUNK):
        cols = slice(c * COL_CHUNK, (c + 1) * COL_CHUNK)
        x1_ref[:, cols] = x_ref[:, cols] + _dot(m, wo_ref[:, cols])
    h2_ref[...] = _rms_norm_rows(x1_ref[...], g2_ref[...]).astype(BF16)
    h2 = h2_ref[...]
    for c in range(D_FF // FF_CHUNK):
        cols = slice(c * FF_CHUNK, (c + 1) * FF_CHUNK)
        gt = _dot(h2, wfi_ref[:, cols])
        up = _dot(h2, wfi_ref[:, D_FF + c * FF_CHUNK:D_FF + (c + 1) * FF_CHUNK])
        act_ref[:, cols] = (gt * _sigmoid(gt) * up).astype(BF16)
    act = act_ref[...]
    for c in range(D_MODEL // COL_CHUNK):
        cols = slice(c * COL_CHUNK, (c + 1) * COL_CHUNK)
        y = x1_ref[:, cols] + _dot(act, wfo_ref[:, cols])
        if final_norm:
            x1_ref[:, cols] = y
        else:
            o_ref[:, cols] = y
    if final_norm:
        o_ref[...] = _rms_norm_rows(x1_ref[...], gf_ref[...])


def _output_stage(x, s, p, gsb, wbs, wo, g2, wfi, wfo, gf, *, tm, final_norm):
    t = x.shape[0]
    full = lambda shape: pl.BlockSpec(shape, lambda i: (0,) * len(shape),
                                      pipeline_mode=pl.Buffered(1))
    rows = lambda width: pl.BlockSpec((tm, width), lambda i: (i, 0))
    return pl.pallas_call(
        functools.partial(_output_stage_kernel, final_norm),
        grid=(t // tm,),
        in_specs=[rows(D_MODEL), rows(SB_WIDTH), rows(D_MODEL), rows(D_MODEL),
                  full((SB_WIDTH, D_MODEL)), full((D_MODEL, D_MODEL)), full((1, D_MODEL)),
                  full((D_MODEL, 2 * D_FF)), full((D_FF, D_MODEL)), full((1, D_MODEL))],
        out_specs=rows(D_MODEL),
        out_shape=jax.ShapeDtypeStruct((t, D_MODEL), F32),
        scratch_shapes=[pltpu.VMEM((tm, D_MODEL), BF16),
                        pltpu.VMEM((tm, D_MODEL), F32),
                        pltpu.VMEM((tm, D_MODEL), BF16),
                        pltpu.VMEM((tm, D_FF), BF16)],
        compiler_params=pltpu.CompilerParams(dimension_semantics=("arbitrary",),
                                             vmem_limit_bytes=VMEM_LIMIT_BYTES),
        name="output_stage",
    )(x, s, p, gsb, wbs, wo, g2, wfi, wfo, gf)


def kernel(x, meta_tokens, norm1_g, w_in, b_gate, pool_mix, pool_scale, w_branch_pool, w_branch_sb,
           w_out, norm2_g, w_ffn_in, w_ffn_out, final_norm_g):
    b, seq, d = x.shape
    depth = w_in.shape[0]
    l = N_META + seq
    lp = -(-l // Q_BLOCK) * Q_BLOCK
    tiles_per_seq = 4
    tm = lp // tiles_per_seq

    meta = jnp.broadcast_to(meta_tokens[None].astype(x.dtype), (b, N_META, d))
    hcat = jnp.concatenate([meta, x, jnp.zeros((b, lp - l, d), x.dtype)], axis=1).reshape(b * lp, d)

    for layer in range(depth):
        q, k, v, p, gsb = _input_stage(
            hcat, norm1_g[layer][None], w_in[layer].astype(BF16), b_gate[layer][None],
            pool_mix[layer].astype(BF16), pool_scale[layer][None], w_branch_pool[layer].astype(BF16),
            tm=tm, tiles_per_seq=tiles_per_seq)
        s = _attention(q.reshape(b, lp, SB_WIDTH), k.reshape(b, lp, SB_WIDTH),
                       v.reshape(b, lp, SB_WIDTH)).reshape(b * lp, SB_WIDTH)
        hcat = _output_stage(
            hcat, s, p, gsb, w_branch_sb[layer].astype(BF16), w_out[layer].astype(BF16),
            norm2_g[layer][None], w_ffn_in[layer].astype(BF16), w_ffn_out[layer].astype(BF16),
            final_norm_g[None], tm=tm, final_norm=(layer == depth - 1))

    return hcat.reshape(b, lp, d)[:, N_META:l]
```

```python
import functools
import math

import numpy as np

import jax
import jax.numpy as jnp
from jax import lax
from jax.experimental import pallas as pl
from jax.experimental.pallas import tpu as pltpu

D_MODEL = 1024
N_META = 16
SB_HEAD_DIM = 64
SB_WIDTH = 512
POOL_WINDOWS = (2, 4, 8, 16)
POOL_WIDTH = 512
POOL_GROUP_DIM = 128
N_IN = 4096
D_FF = 2816
RMS_EPS = 1e-6

LANES = 128
Q_BLOCK = 128
K_BLOCK = 256
ATTN_SLOTS = 4
HALO = 16
FF_CHUNK = 256
COL_CHUNK = 512
VMEM_LIMIT_BYTES = 56 * 1024 * 1024
MASKED_LOGIT = -1e30
LOG2E = math.log2(math.e)

BF16 = jnp.bfloat16
F32 = jnp.float32


def _dot(a, b):
    return jnp.dot(a, b, preferred_element_type=F32)


def _rms_norm_rows(x, g):
    ms = jnp.mean(x * x, axis=-1, keepdims=True)
    return x * lax.rsqrt(ms + RMS_EPS) * g


def _sigmoid(x):
    return 1.0 / (1.0 + jnp.exp(-x))


def _input_stage_kernel(tiles_per_seq, x_ref, g_ref, w_ref, bg_ref, mix_ref, ps_ref, wbp_ref,
                        q_ref, k_ref, v_ref, p_ref, gsb_ref,
                        h_ref, ubuf_ref, a_ref):
    tm = x_ref.shape[0]
    tile_in_seq = pl.program_id(0) % tiles_per_seq

    h_ref[...] = _rms_norm_rows(x_ref[...], g_ref[...]).astype(BF16)

    @pl.when(tile_in_seq == 0)
    def _():
        ubuf_ref[0:HALO, :] = jnp.zeros((HALO, POOL_WIDTH), F32)

    @pl.when(tile_in_seq != 0)
    def _():
        ubuf_ref[0:HALO, :] = ubuf_ref[tm:tm + HALO, :]

    h = h_ref[...]
    q_ref[...] = (_dot(h, w_ref[:, 0:SB_WIDTH]) * (LOG2E * SB_HEAD_DIM ** -0.5)).astype(BF16)
    k_ref[...] = _dot(h, w_ref[:, SB_WIDTH:2 * SB_WIDTH]).astype(BF16)
    v_ref[...] = _dot(h, w_ref[:, 2 * SB_WIDTH:3 * SB_WIDTH]).astype(BF16)
    ubuf_ref[HALO:HALO + tm, :] = _dot(h, w_ref[:, 3 * SB_WIDTH:3 * SB_WIDTH + POOL_WIDTH])

    pos = tile_in_seq * tm + lax.broadcasted_iota(jnp.int32, (tm, 1), 0)
    for g, window in enumerate(POOL_WINDOWS):
        cols = slice(g * POOL_GROUP_DIM, (g + 1) * POOL_GROUP_DIM)
        ug = ubuf_ref[HALO:HALO + tm, cols]
        total = ug
        for d in range(1, window):
            total = total + ubuf_ref[HALO - d:HALO - d + tm, cols]
        cnt = jnp.minimum(pos + 1, window).astype(F32)
        diff = (total / cnt - ug).astype(BF16)
        a_ref[:, cols] = (_dot(diff, mix_ref[g]) * ps_ref[:, cols]).astype(BF16)

    gate0 = 3 * SB_WIDTH + POOL_WIDTH
    a = a_ref[...]
    for c in range(D_MODEL // COL_CHUNK):
        cols = slice(c * COL_CHUNK, (c + 1) * COL_CHUNK)
        g_pool = _sigmoid(_dot(h, w_ref[:, gate0 + c * COL_CHUNK:gate0 + (c + 1) * COL_CHUNK])
                          + bg_ref[:, cols])
        p_ref[:, cols] = g_pool * _dot(a, wbp_ref[:, cols])
        s0 = gate0 + D_MODEL + c * COL_CHUNK
        gsb_ref[:, cols] = _sigmoid(_dot(h, w_ref[:, s0:s0 + COL_CHUNK])
                                    + bg_ref[:, D_MODEL + c * COL_CHUNK:D_MODEL + (c + 1) * COL_CHUNK])


def _input_stage(x, g, w, bg, mix, ps, wbp, *, tm, tiles_per_seq):
    t = x.shape[0]
    full = lambda shape: pl.BlockSpec(shape, lambda i: (0,) * len(shape))
    rows = lambda width: pl.BlockSpec((tm, width), lambda i: (i, 0))
    return pl.pallas_call(
        functools.partial(_input_stage_kernel, tiles_per_seq),
        grid=(t // tm,),
        in_specs=[rows(D_MODEL), full((1, D_MODEL)), full((D_MODEL, N_IN)), full((1, 2 * D_MODEL)),
                  full((len(POOL_WINDOWS), POOL_GROUP_DIM, POOL_GROUP_DIM)), full((1, POOL_WIDTH)),
                  full((POOL_WIDTH, D_MODEL))],
        out_specs=[rows(SB_WIDTH), rows(SB_WIDTH), rows(SB_WIDTH), rows(D_MODEL), rows(D_MODEL)],
        out_shape=[jax.ShapeDtypeStruct((t, SB_WIDTH), BF16)] * 3
                  + [jax.ShapeDtypeStruct((t, D_MODEL), F32)] * 2,
        scratch_shapes=[pltpu.VMEM((tm, D_MODEL), BF16),
                        pltpu.VMEM((tm + HALO, POOL_WIDTH), F32),
                        pltpu.VMEM((tm, POOL_WIDTH), BF16)],
        compiler_params=pltpu.CompilerParams(dimension_semantics=("arbitrary",),
                                             vmem_limit_bytes=VMEM_LIMIT_BYTES),
        name="input_stage",
    )(x, g, w, bg, mix, ps, wbp)


def _attention_schedule(lp):
    kinds = {None: 0}
    blocks = []
    for i in range(lp // Q_BLOCK):
        n = i * Q_BLOCK // K_BLOCK + 1
        for j in range(n):
            if j < n - 1:
                blocks.append((i, j * K_BLOCK, 0))
                continue
            start = min(j * K_BLOCK, lp - K_BLOCK)
            key = (j * K_BLOCK - start, i * Q_BLOCK - start)
            blocks.append((i, start, kinds.setdefault(key, len(kinds))))
    r = (np.arange(2 * Q_BLOCK) % Q_BLOCK)[:, None]
    c = np.arange(K_BLOCK)[None, :]
    bias = np.zeros((len(kinds), 2 * Q_BLOCK, K_BLOCK), np.float32)
    for key, kind in kinds.items():
        if key is not None:
            lo, off = key
            bias[kind] = np.where((c >= lo) & (c < r + off), 0.0, MASKED_LOGIT)
    return np.asarray(blocks, np.int32), bias


def _attention_kernel(n_blocks, sched_ref, q_ref, k_ref, v_ref, tri_ref, bias_ref, o_ref,
                      qs_ref, acc_ref, z_ref, sp_ref, incl_ref, w_ref):
    n_q = qs_ref.shape[0]
    lane = lax.broadcasted_iota(jnp.int32, (Q_BLOCK, LANES), 1)
    for i in range(n_q):
        q2 = q_ref[0, i * Q_BLOCK:(i + 1) * Q_BLOCK, :]
        zero = jnp.zeros_like(q2)
        qs_ref[i, 0:Q_BLOCK, :] = jnp.where(lane < SB_HEAD_DIM, q2, zero)
        qs_ref[i, Q_BLOCK:2 * Q_BLOCK, :] = jnp.where(lane >= SB_HEAD_DIM, q2, zero)
    acc_ref[...] = jnp.zeros_like(acc_ref)

    def block(t):
        return (sched_ref[t], pl.multiple_of(sched_ref[n_blocks + t], Q_BLOCK),
                sched_ref[2 * n_blocks + t])

    def raw_scores(t, slot):
        i, start, _ = block(t)
        kb = k_ref[0, pl.ds(start, K_BLOCK), :]
        z_ref[slot] = lax.dot_general(qs_ref[i], kb, (((1,), (1,)), ((), ())),
                                      preferred_element_type=F32)

    def softplus(t, slot):
        z2 = z_ref[slot] + bias_ref[block(t)[2]]
        sp2 = jnp.maximum(z2, 0.0) + jnp.log(1.0 + jnp.exp2(-jnp.abs(z2))) * LOG2E
        hi = sp2.astype(BF16)
        z_ref[slot] = z2
        sp_ref[slot, :, 0:K_BLOCK] = hi
        sp_ref[slot, :, K_BLOCK:2 * K_BLOCK] = (sp2 - hi.astype(F32)).astype(BF16)

    def suffix_sums(t, slot):
        incl_ref[slot] = _dot(sp_ref[slot], tri_ref[...])

    def weights(t, slot):
        w_ref[slot] = jnp.exp2(z_ref[slot] + incl_ref[slot]).astype(BF16)

    def accumulate(t, slot):
        i, start, _ = block(t)
        decay = jnp.exp2(incl_ref[slot, :, 0:1])
        pv = _dot(w_ref[slot], v_ref[0, pl.ds(start, K_BLOCK), :])
        acc_ref[i] = acc_ref[i] * decay + pv

    stages = (raw_scores, softplus, suffix_sums, weights, accumulate)
    depth = len(stages) - 1

    def trip(t, t_mod_slots):
        for s in range(depth, -1, -1):
            if isinstance(t, int) and not 0 <= t - s < n_blocks:
                continue
            stages[s](t - s, (t_mod_slots - s) % ATTN_SLOTS)

    for t in range(depth):
        trip(t, t % ATTN_SLOTS)

    n_groups = (n_blocks - depth) // ATTN_SLOTS

    def body(g, c):
        t0 = depth + g * ATTN_SLOTS
        for u in range(ATTN_SLOTS):
            trip(t0 + u, (depth + u) % ATTN_SLOTS)
        return c

    lax.fori_loop(0, n_groups, body, 0)

    for t in range(depth + n_groups * ATTN_SLOTS, n_blocks + depth):
        trip(t, t % ATTN_SLOTS)

    for i in range(n_q):
        acc = acc_ref[i]
        o_ref[0, i * Q_BLOCK:(i + 1) * Q_BLOCK, :] = jnp.where(
            lane < SB_HEAD_DIM, acc[:Q_BLOCK], acc[Q_BLOCK:]).astype(o_ref.dtype)


def _attention(q, k, v):
    b, lp, _ = q.shape
    n_pairs = SB_WIDTH // LANES
    n_q = lp // Q_BLOCK
    blocks, bias = _attention_schedule(lp)
    n_blocks = blocks.shape[0]
    sched = jnp.asarray(blocks.T.reshape(-1))
    j = np.arange(K_BLOCK)[:, None]
    s = np.arange(K_BLOCK)[None, :]
    tri = np.where(j >= s, -1.0, 0.0).astype(np.float32)
    tri = jnp.asarray(np.concatenate([tri, tri], axis=0), BF16)
    seq = lambda: pl.BlockSpec((1, lp, LANES), lambda bi, hp, sched: (bi, 0, hp))
    return pl.pallas_call(
        functools.partial(_attention_kernel, n_blocks),
        grid_spec=pltpu.PrefetchScalarGridSpec(
            num_scalar_prefetch=1,
            grid=(b, n_pairs),
            in_specs=[seq(), seq(), seq(),
                      pl.BlockSpec(tri.shape, lambda bi, hp, sched: (0, 0)),
                      pl.BlockSpec(bias.shape, lambda bi, hp, sched: (0, 0, 0))],
            out_specs=seq(),
            scratch_shapes=[pltpu.VMEM((n_q, 2 * Q_BLOCK, LANES), BF16),
                            pltpu.VMEM((n_q, 2 * Q_BLOCK, LANES), F32),
                            pltpu.VMEM((ATTN_SLOTS, 2 * Q_BLOCK, K_BLOCK), F32),
                            pltpu.VMEM((ATTN_SLOTS, 2 * Q_BLOCK, 2 * K_BLOCK), BF16),
                            pltpu.VMEM((ATTN_SLOTS, 2 * Q_BLOCK, K_BLOCK), F32),
                            pltpu.VMEM((ATTN_SLOTS, 2 * Q_BLOCK, K_BLOCK), BF16)]),
        out_shape=jax.ShapeDtypeStruct((b, lp, SB_WIDTH), BF16),
        compiler_params=pltpu.CompilerParams(dimension_semantics=("arbitrary", "arbitrary")),
        name="stick_breaking_attention",
    )(sched, q, k, v, tri, jnp.asarray(bias))


def _output_stage_kernel(final_norm, x_ref, s_ref, p_ref, gsb_ref, wbs_ref, wo_ref, g2_ref,
                         wfi_ref, wfo_ref, gf_ref, o_ref, m_ref, x1_ref, h2_ref, act_ref):
    s = s_ref[...]
    for c in range(D_MODEL // COL_CHUNK):
        cols = slice(c * COL_CHUNK, (c + 1) * COL_CHUNK)
        m_ref[:, cols] = (p_ref[:, cols] + gsb_ref[:, cols] * _dot(s, wbs_ref[:, cols])).astype(BF16)
    m = m_ref[...]
    for c in range(D_MODEL // COL_CHUNK):
        cols = slice(c * COL_CHUNK, (c + 1) * COL_CHUNK)
        x1_ref[:, cols] = x_ref[:, cols] + _dot(m, wo_ref[:, cols])
    h2_ref[...] = _rms_norm_rows(x1_ref[...], g2_ref[...]).astype(BF16)
    h2 = h2_ref[...]
    for c in range(D_FF // FF_CHUNK):
        cols = slice(c * FF_CHUNK, (c + 1) * FF_CHUNK)
        gt = _dot(h2, wfi_ref[:, cols])
        up = _dot(h2, wfi_ref[:, D_FF + c * FF_CHUNK:D_FF + (c + 1) * FF_CHUNK])
        act_ref[:, cols] = (gt * _sigmoid(gt) * up).astype(BF16)
    act = act_ref[...]
    for c in range(D_MODEL // COL_CHUNK):
        cols = slice(c * COL_CHUNK, (c + 1) * COL_CHUNK)
        y = x1_ref[:, cols] + _dot(act, wfo_ref[:, cols])
        if final_norm:
            x1_ref[:, cols] = y
        else:
            o_ref[:, cols] = y
    if final_norm:
        o_ref[...] = _rms_norm_rows(x1_ref[...], gf_ref[...])


def _output_stage(x, s, p, gsb, wbs, wo, g2, wfi, wfo, gf, *, tm, final_norm):
    t = x.shape[0]
    full = lambda shape: pl.BlockSpec(shape, lambda i: (0,) * len(shape),
                                      pipeline_mode=pl.Buffered(1))
    rows = lambda width: pl.BlockSpec((tm, width), lambda i: (i, 0))
    return pl.pallas_call(
        functools.partial(_output_stage_kernel, final_norm),
        grid=(t // tm,),
        in_specs=[rows(D_MODEL), rows(SB_WIDTH), rows(D_MODEL), rows(D_MODEL),
                  full((SB_WIDTH, D_MODEL)), full((D_MODEL, D_MODEL)), full((1, D_MODEL)),
                  full((D_MODEL, 2 * D_FF)), full((D_FF, D_MODEL)), full((1, D_MODEL))],
        out_specs=rows(D_MODEL),
        out_shape=jax.ShapeDtypeStruct((t, D_MODEL), F32),
        scratch_shapes=[pltpu.VMEM((tm, D_MODEL), BF16),
                        pltpu.VMEM((tm, D_MODEL), F32),
                        pltpu.VMEM((tm, D_MODEL), BF16),
                        pltpu.VMEM((tm, D_FF), BF16)],
        compiler_params=pltpu.CompilerParams(dimension_semantics=("arbitrary",),
                                             vmem_limit_bytes=VMEM_LIMIT_BYTES),
        name="output_stage",
    )(x, s, p, gsb, wbs, wo, g2, wfi, wfo, gf)


def kernel(x, meta_tokens, norm1_g, w_in, b_gate, pool_mix, pool_scale, w_branch_pool, w_branch_sb,
           w_out, norm2_g, w_ffn_in, w_ffn_out, final_norm_g):
    b, seq, d = x.shape
    depth = w_in.shape[0]
    l = N_META + seq
    lp = -(-l // Q_BLOCK) * Q_BLOCK
    tiles_per_seq = 4
    tm = lp // tiles_per_seq

    meta = jnp.broadcast_to(meta_tokens[None].astype(x.dtype), (b, N_META, d))
    hcat = jnp.concatenate([meta, x, jnp.zeros((b, lp - l, d), x.dtype)], axis=1).reshape(b * lp, d)

    for layer in range(depth):
        q, k, v, p, gsb = _input_stage(
            hcat, norm1_g[layer][None], w_in[layer].astype(BF16), b_gate[layer][None],
            pool_mix[layer].astype(BF16), pool_scale[layer][None], w_branch_pool[layer].astype(BF16),
            tm=tm, tiles_per_seq=tiles_per_seq)
        s = _attention(q.reshape(b, lp, SB_WIDTH), k.reshape(b, lp, SB_WIDTH),
                       v.reshape(b, lp, SB_WIDTH)).reshape(b * lp, SB_WIDTH)
        hcat = _output_stage(
            hcat, s, p, gsb, w_branch_sb[layer].astype(BF16), w_out[layer].astype(BF16),
            norm2_g[layer][None], w_ffn_in[layer].astype(BF16), w_ffn_out[layer].astype(BF16),
            final_norm_g[None], tm=tm, final_norm=(layer == depth - 1))

    return hcat.reshape(b, lp, d)[:, N_META:l]
```

```python
import functools
import math

import numpy as np

import jax
import jax.numpy as jnp
from jax import lax
from jax.experimental import pallas as pl
from jax.experimental.pallas import tpu as pltpu

D_MODEL = 1024
N_META = 16
SB_HEAD_DIM = 64
SB_WIDTH = 512
POOL_WINDOWS = (2, 4, 8, 16)
POOL_WIDTH = 512
POOL_GROUP_DIM = 128
N_IN = 4096
D_FF = 2816
RMS_EPS = 1e-6

LANES = 128
Q_BLOCK = 128
K_BLOCK = 256
ATTN_SLOTS = 4
HALO = 16
FF_CHUNK = 256
COL_CHUNK = 512
VMEM_LIMIT_BYTES = 56 * 1024 * 1024
MASKED_LOGIT = -1e30
LOG2E = math.log2(math.e)

BF16 = jnp.bfloat16
F32 = jnp.float32


def _dot(a, b):
    return jnp.dot(a, b, preferred_element_type=F32)


def _rms_norm_rows(x, g):
    ms = jnp.mean(x * x, axis=-1, keepdims=True)
    return x * lax.rsqrt(ms + RMS_EPS) * g


def _sigmoid(x):
    return 1.0 / (1.0 + jnp.exp(-x))


def _input_stage_kernel(tiles_per_seq, x_ref, g_ref, w_ref, bg_ref, mix_ref, ps_ref, wbp_ref,
                        q_ref, k_ref, v_ref, p_ref, gsb_ref,
                        h_ref, ubuf_ref, a_ref):
    tm = x_ref.shape[0]
    tile_in_seq = pl.program_id(0) % tiles_per_seq

    h_ref[...] = _rms_norm_rows(x_ref[...], g_ref[...]).astype(BF16)

    @pl.when(tile_in_seq == 0)
    def _():
        ubuf_ref[0:HALO, :] = jnp.zeros((HALO, POOL_WIDTH), F32)

    @pl.when(tile_in_seq != 0)
    def _():
        ubuf_ref[0:HALO, :] = ubuf_ref[tm:tm + HALO, :]

    h = h_ref[...]
    q_ref[...] = (_dot(h, w_ref[:, 0:SB_WIDTH]) * (LOG2E * SB_HEAD_DIM ** -0.5)).astype(BF16)
    k_ref[...] = _dot(h, w_ref[:, SB_WIDTH:2 * SB_WIDTH]).astype(BF16)
    v_ref[...] = _dot(h, w_ref[:, 2 * SB_WIDTH:3 * SB_WIDTH]).astype(BF16)
    ubuf_ref[HALO:HALO + tm, :] = _dot(h, w_ref[:, 3 * SB_WIDTH:3 * SB_WIDTH + POOL_WIDTH])

    pos = tile_in_seq * tm + lax.broadcasted_iota(jnp.int32, (tm, 1), 0)
    for g, window in enumerate(POOL_WINDOWS):
        cols = slice(g * POOL_GROUP_DIM, (g + 1) * POOL_GROUP_DIM)
        ug = ubuf_ref[HALO:HALO + tm, cols]
        total = ug
        for d in range(1, window):
            total = total + ubuf_ref[HALO - d:HALO - d + tm, cols]
        cnt = jnp.minimum(pos + 1, window).astype(F32)
        diff = (total / cnt - ug).astype(BF16)
        a_ref[:, cols] = (_dot(diff, mix_ref[g]) * ps_ref[:, cols]).astype(BF16)

    gate0 = 3 * SB_WIDTH + POOL_WIDTH
    a = a_ref[...]
    for c in range(D_MODEL // COL_CHUNK):
        cols = slice(c * COL_CHUNK, (c + 1) * COL_CHUNK)
        g_pool = _sigmoid(_dot(h, w_ref[:, gate0 + c * COL_CHUNK:gate0 + (c + 1) * COL_CHUNK])
                          + bg_ref[:, cols])
        p_ref[:, cols] = g_pool * _dot(a, wbp_ref[:, cols])
        s0 = gate0 + D_MODEL + c * COL_CHUNK
        gsb_ref[:, cols] = _sigmoid(_dot(h, w_ref[:, s0:s0 + COL_CHUNK])
                                    + bg_ref[:, D_MODEL + c * COL_CHUNK:D_MODEL + (c + 1) * COL_CHUNK])


def _layer_spec(layer, shape, **kwargs):
    return pl.BlockSpec((None,) + shape, lambda i: (layer,) + (0,) * len(shape), **kwargs)


def _input_stage(x, g, w, bg, mix, ps, wbp, *, layer, tm, tiles_per_seq):
    t = x.shape[0]
    full = functools.partial(_layer_spec, layer)
    rows = lambda width: pl.BlockSpec((tm, width), lambda i: (i, 0))
    return pl.pallas_call(
        functools.partial(_input_stage_kernel, tiles_per_seq),
        grid=(t // tm,),
        in_specs=[rows(D_MODEL), full((1, D_MODEL)), full((D_MODEL, N_IN)), full((1, 2 * D_MODEL)),
                  full((len(POOL_WINDOWS), POOL_GROUP_DIM, POOL_GROUP_DIM)), full((1, POOL_WIDTH)),
                  full((POOL_WIDTH, D_MODEL))],
        out_specs=[rows(SB_WIDTH), rows(SB_WIDTH), rows(SB_WIDTH), rows(D_MODEL), rows(D_MODEL)],
        out_shape=[jax.ShapeDtypeStruct((t, SB_WIDTH), BF16)] * 3
                  + [jax.ShapeDtypeStruct((t, D_MODEL), F32)] * 2,
        scratch_shapes=[pltpu.VMEM((tm, D_MODEL), BF16),
                        pltpu.VMEM((tm + HALO, POOL_WIDTH), F32),
                        pltpu.VMEM((tm, POOL_WIDTH), BF16)],
        compiler_params=pltpu.CompilerParams(dimension_semantics=("arbitrary",),
                                             vmem_limit_bytes=VMEM_LIMIT_BYTES),
        name="input_stage",
    )(x, g, w, bg, mix, ps, wbp)


def _attention_schedule(lp):
    kinds = {None: 0}
    blocks = []
    for i in range(lp // Q_BLOCK):
        n = i * Q_BLOCK // K_BLOCK + 1
        for j in range(n):
            if j < n - 1:
                blocks.append((i, j * K_BLOCK, 0))
                continue
            start = min(j * K_BLOCK, lp - K_BLOCK)
            key = (j * K_BLOCK - start, i * Q_BLOCK - start)
            blocks.append((i, start, kinds.setdefault(key, len(kinds))))
    r = (np.arange(2 * Q_BLOCK) % Q_BLOCK)[:, None]
    c = np.arange(K_BLOCK)[None, :]
    bias = np.zeros((len(kinds), 2 * Q_BLOCK, K_BLOCK), np.float32)
    for key, kind in kinds.items():
        if key is not None:
            lo, off = key
            bias[kind] = np.where((c >= lo) & (c < r + off), 0.0, MASKED_LOGIT)
    return np.asarray(blocks, np.int32), bias


def _attention_kernel(n_blocks, sched_ref, q_ref, k_ref, v_ref, tri_ref, bias_ref, o_ref,
                      qs_ref, acc_ref, z_ref, sp_ref, incl_ref, w_ref):
    n_q = qs_ref.shape[0]
    lane = lax.broadcasted_iota(jnp.int32, (Q_BLOCK, LANES), 1)
    for i in range(n_q):
        q2 = q_ref[0, i * Q_BLOCK:(i + 1) * Q_BLOCK, :]
        zero = jnp.zeros_like(q2)
        qs_ref[i, 0:Q_BLOCK, :] = jnp.where(lane < SB_HEAD_DIM, q2, zero)
        qs_ref[i, Q_BLOCK:2 * Q_BLOCK, :] = jnp.where(lane >= SB_HEAD_DIM, q2, zero)
    acc_ref[...] = jnp.zeros_like(acc_ref)

    def block(t):
        return (sched_ref[t], pl.multiple_of(sched_ref[n_blocks + t], Q_BLOCK),
                sched_ref[2 * n_blocks + t])

    def raw_scores(t, slot):
        i, start, _ = block(t)
        kb = k_ref[0, pl.ds(start, K_BLOCK), :]
        z_ref[slot] = lax.dot_general(qs_ref[i], kb, (((1,), (1,)), ((), ())),
                                      preferred_element_type=F32)

    def softplus(t, slot):
        z2 = z_ref[slot] + bias_ref[block(t)[2]]
        sp2 = jnp.maximum(z2, 0.0) + jnp.log(1.0 + jnp.exp2(-jnp.abs(z2))) * LOG2E
        z_ref[slot] = z2
        sp_ref[slot] = sp2.astype(BF16)

    def suffix_sums(t, slot):
        incl_ref[slot] = _dot(sp_ref[slot], tri_ref[...])

    def weights(t, slot):
        w_ref[slot] = jnp.exp2(z_ref[slot] + incl_ref[slot]).astype(BF16)

    def accumulate(t, slot):
        i, start, _ = block(t)
        decay = jnp.exp2(incl_ref[slot, :, 0:1])
        pv = _dot(w_ref[slot], v_ref[0, pl.ds(start, K_BLOCK), :])
        acc_ref[i] = acc_ref[i] * decay + pv

    stages = (raw_scores, softplus, suffix_sums, weights, accumulate)
    depth = len(stages) - 1

    def trip(t, t_mod_slots):
        for s in range(depth, -1, -1):
            if isinstance(t, int) and not 0 <= t - s < n_blocks:
                continue
            stages[s](t - s, (t_mod_slots - s) % ATTN_SLOTS)

    for t in range(depth):
        trip(t, t % ATTN_SLOTS)

    n_groups = (n_blocks - depth) // ATTN_SLOTS

    def body(g, c):
        t0 = depth + g * ATTN_SLOTS
        for u in range(ATTN_SLOTS):
            trip(t0 + u, (depth + u) % ATTN_SLOTS)
        return c

    lax.fori_loop(0, n_groups, body, 0)

    for t in range(depth + n_groups * ATTN_SLOTS, n_blocks + depth):
        trip(t, t % ATTN_SLOTS)

    for i in range(n_q):
        acc = acc_ref[i]
        o_ref[0, i * Q_BLOCK:(i + 1) * Q_BLOCK, :] = jnp.where(
            lane < SB_HEAD_DIM, acc[:Q_BLOCK], acc[Q_BLOCK:]).astype(o_ref.dtype)


def _attention(q, k, v):
    b, lp, _ = q.shape
    n_pairs = SB_WIDTH // LANES
    n_q = lp // Q_BLOCK
    blocks, bias = _attention_schedule(lp)
    n_blocks = blocks.shape[0]
    sched = jnp.asarray(blocks.T.reshape(-1))
    j = np.arange(K_BLOCK)[:, None]
    s = np.arange(K_BLOCK)[None, :]
    tri = jnp.asarray(np.where(j >= s, -1.0, 0.0), BF16)
    seq = lambda: pl.BlockSpec((1, lp, LANES), lambda bi, hp, sched: (bi, 0, hp))
    return pl.pallas_call(
        functools.partial(_attention_kernel, n_blocks),
        grid_spec=pltpu.PrefetchScalarGridSpec(
            num_scalar_prefetch=1,
            grid=(b, n_pairs),
            in_specs=[seq(), seq(), seq(),
                      pl.BlockSpec(tri.shape, lambda bi, hp, sched: (0, 0)),
                      pl.BlockSpec(bias.shape, lambda bi, hp, sched: (0, 0, 0))],
            out_specs=seq(),
            scratch_shapes=[pltpu.VMEM((n_q, 2 * Q_BLOCK, LANES), BF16),
                            pltpu.VMEM((n_q, 2 * Q_BLOCK, LANES), F32),
                            pltpu.VMEM((ATTN_SLOTS, 2 * Q_BLOCK, K_BLOCK), F32),
                            pltpu.VMEM((ATTN_SLOTS, 2 * Q_BLOCK, K_BLOCK), BF16),
                            pltpu.VMEM((ATTN_SLOTS, 2 * Q_BLOCK, K_BLOCK), F32),
                            pltpu.VMEM((ATTN_SLOTS, 2 * Q_BLOCK, K_BLOCK), BF16)]),
        out_shape=jax.ShapeDtypeStruct((b, lp, SB_WIDTH), BF16),
        compiler_params=pltpu.CompilerParams(dimension_semantics=("arbitrary", "arbitrary")),
        name="stick_breaking_attention",
    )(sched, q, k, v, tri, jnp.asarray(bias))


def _output_stage_kernel(final_norm, x_ref, s_ref, p_ref, gsb_ref, wbs_ref, wo_ref, g2_ref,
                         wfi_ref, wfo_ref, gf_ref, o_ref, m_ref, x1_ref, h2_ref, act_ref):
    s = s_ref[...]
    for c in range(D_MODEL // COL_CHUNK):
        cols = slice(c * COL_CHUNK, (c + 1) * COL_CHUNK)
        m_ref[:, cols] = (p_ref[:, cols] + gsb_ref[:, cols] * _dot(s, wbs_ref[:, cols])).astype(BF16)
    m = m_ref[...]
    for c in range(D_MODEL // COL_CHUNK):
        cols = slice(c * COL_CHUNK, (c + 1) * COL_CHUNK)
        x1_ref[:, cols] = x_ref[:, cols] + _dot(m, wo_ref[:, cols])
    h2_ref[...] = _rms_norm_rows(x1_ref[...], g2_ref[...]).astype(BF16)
    h2 = h2_ref[...]
    for c in range(D_FF // FF_CHUNK):
        cols = slice(c * FF_CHUNK, (c + 1) * FF_CHUNK)
        gt = _dot(h2, wfi_ref[:, cols])
        up = _dot(h2, wfi_ref[:, D_FF + c * FF_CHUNK:D_FF + (c + 1) * FF_CHUNK])
        act_ref[:, cols] = (gt * _sigmoid(gt) * up).astype(BF16)
    act = act_ref[...]
    for c in range(D_MODEL // COL_CHUNK):
        cols = slice(c * COL_CHUNK, (c + 1) * COL_CHUNK)
        y = x1_ref[:, cols] + _dot(act, wfo_ref[:, cols])
        if final_norm:
            x1_ref[:, cols] = y
        else:
            o_ref[:, cols] = y
    if final_norm:
        o_ref[...] = _rms_norm_rows(x1_ref[...], gf_ref[...])


def _output_stage(x, s, p, gsb, wbs, wo, g2, wfi, wfo, gf, *, layer, tm, final_norm):
    t = x.shape[0]
    full = functools.partial(_layer_spec, layer, pipeline_mode=pl.Buffered(1))
    rows = lambda width: pl.BlockSpec((tm, width), lambda i: (i, 0))
    return pl.pallas_call(
        functools.partial(_output_stage_kernel, final_norm),
        grid=(t // tm,),
        in_specs=[rows(D_MODEL), rows(SB_WIDTH), rows(D_MODEL), rows(D_MODEL),
                  full((SB_WIDTH, D_MODEL)), full((D_MODEL, D_MODEL)), full((1, D_MODEL)),
                  full((D_MODEL, 2 * D_FF)), full((D_FF, D_MODEL)),
                  pl.BlockSpec((1, D_MODEL), lambda i: (0, 0))],
        out_specs=rows(D_MODEL),
        out_shape=jax.ShapeDtypeStruct((t, D_MODEL), F32),
        scratch_shapes=[pltpu.VMEM((tm, D_MODEL), BF16),
                        pltpu.VMEM((tm, D_MODEL), F32),
                        pltpu.VMEM((tm, D_MODEL), BF16),
                        pltpu.VMEM((tm, D_FF), BF16)],
        compiler_params=pltpu.CompilerParams(dimension_semantics=("arbitrary",),
                                             vmem_limit_bytes=VMEM_LIMIT_BYTES),
        name="output_stage",
    )(x, s, p, gsb, wbs, wo, g2, wfi, wfo, gf)


def kernel(x, meta_tokens, norm1_g, w_in, b_gate, pool_mix, pool_scale, w_branch_pool, w_branch_sb,
           w_out, norm2_g, w_ffn_in, w_ffn_out, final_norm_g):
    b, seq, d = x.shape
    depth = w_in.shape[0]
    l = N_META + seq
    lp = -(-l // Q_BLOCK) * Q_BLOCK
    tiles_per_seq = 4
    tm = lp // tiles_per_seq

    meta = jnp.broadcast_to(meta_tokens[None].astype(x.dtype), (b, N_META, d))
    hcat = jnp.concatenate([meta, x, jnp.zeros((b, lp - l, d), x.dtype)], axis=1).reshape(b * lp, d)

    w_in, pool_mix, w_branch_pool, w_branch_sb, w_out, w_ffn_in, w_ffn_out = (
        w.astype(BF16) for w in (w_in, pool_mix, w_branch_pool, w_branch_sb, w_out, w_ffn_in, w_ffn_out))
    norm1_g, b_gate, pool_scale, norm2_g = (
        a[:, None, :] for a in (norm1_g, b_gate, pool_scale, norm2_g))

    for layer in range(depth):
        q, k, v, p, gsb = _input_stage(
            hcat, norm1_g, w_in, b_gate, pool_mix, pool_scale, w_branch_pool,
            layer=layer, tm=tm, tiles_per_seq=tiles_per_seq)
        s = _attention(q.reshape(b, lp, SB_WIDTH), k.reshape(b, lp, SB_WIDTH),
                       v.reshape(b, lp, SB_WIDTH)).reshape(b * lp, SB_WIDTH)
        hcat = _output_stage(
            hcat, s, p, gsb, w_branch_sb, w_out, norm2_g, w_ffn_in, w_ffn_out, final_norm_g[None],
            layer=layer, tm=tm, final_norm=(layer == depth - 1))

    return hcat.reshape(b, lp, d)[:, N_META:l]
```

```python
import functools
import math

import numpy as np

import jax
import jax.numpy as jnp
from jax import lax
from jax.experimental import pallas as pl
from jax.experimental.pallas import tpu as pltpu

D_MODEL = 1024
N_META = 16
SB_HEAD_DIM = 64
SB_WIDTH = 512
POOL_WINDOWS = (2, 4, 8, 16)
POOL_WIDTH = 512
POOL_GROUP_DIM = 128
N_IN = 4096
D_FF = 2816
RMS_EPS = 1e-6

LANES = 128
Q_BLOCK = 128
K_BLOCK = 256
ATTN_SLOTS = 8
HALO = 16
FF_CHUNK = 256
COL_CHUNK = 512
VMEM_LIMIT_BYTES = 56 * 1024 * 1024
MASKED_LOGIT = -1e30

BF16 = jnp.bfloat16
F32 = jnp.float32


def _dot(a, b):
    return jnp.dot(a, b, preferred_element_type=F32)


def _rms_norm_rows(x, g):
    ms = jnp.mean(x * x, axis=-1, keepdims=True)
    return x * lax.rsqrt(ms + RMS_EPS) * g


def _sigmoid(x):
    return 1.0 / (1.0 + jnp.exp(-x))


def _input_stage_kernel(tiles_per_seq, x_ref, g_ref, w_ref, bg_ref, mix_ref, ps_ref, wbp_ref,
                        q_ref, k_ref, v_ref, p_ref, gsb_ref,
                        h_ref, ubuf_ref, a_ref):
    tm = x_ref.shape[0]
    tile_in_seq = pl.program_id(0) % tiles_per_seq

    h_ref[...] = _rms_norm_rows(x_ref[...], g_ref[...]).astype(BF16)

    @pl.when(tile_in_seq == 0)
    def _():
        ubuf_ref[0:HALO, :] = jnp.zeros((HALO, POOL_WIDTH), F32)

    @pl.when(tile_in_seq != 0)
    def _():
        ubuf_ref[0:HALO, :] = ubuf_ref[tm:tm + HALO, :]

    h = h_ref[...]
    ubuf_ref[HALO:HALO + tm, :] = _dot(h, w_ref[:, 3 * SB_WIDTH:3 * SB_WIDTH + POOL_WIDTH])

    def gate(branch_ref, c):
        first = D_MODEL * (branch_ref is gsb_ref) + c * COL_CHUNK
        w0 = 3 * SB_WIDTH + POOL_WIDTH + first
        branch_ref[:, c * COL_CHUNK:(c + 1) * COL_CHUNK] = _sigmoid(
            _dot(h, w_ref[:, w0:w0 + COL_CHUNK]) + bg_ref[:, first:first + COL_CHUNK])

    gates = [(ref, c) for c in range(D_MODEL // COL_CHUNK) for ref in (p_ref, gsb_ref)]
    pos = tile_in_seq * tm + lax.broadcasted_iota(jnp.int32, (tm, 1), 0)
    for g, window in enumerate(POOL_WINDOWS):
        gate(*gates[g])
        cols = slice(g * POOL_GROUP_DIM, (g + 1) * POOL_GROUP_DIM)
        ug = ubuf_ref[HALO:HALO + tm, cols]
        total = ug
        for d in range(1, window):
            total = total + ubuf_ref[HALO - d:HALO - d + tm, cols]
        cnt = jnp.minimum(pos + 1, window).astype(F32)
        diff = (total / cnt - ug).astype(BF16)
        a_ref[:, cols] = (_dot(diff, mix_ref[g]) * ps_ref[:, cols]).astype(BF16)
    for extra in gates[len(POOL_WINDOWS):]:
        gate(*extra)

    q_ref[...] = (_dot(h, w_ref[:, 0:SB_WIDTH]) * (SB_HEAD_DIM ** -0.5)).astype(BF16)
    k_ref[...] = _dot(h, w_ref[:, SB_WIDTH:2 * SB_WIDTH]).astype(BF16)
    v_ref[...] = _dot(h, w_ref[:, 2 * SB_WIDTH:3 * SB_WIDTH]).astype(BF16)

    a = a_ref[...]
    for c in range(D_MODEL // COL_CHUNK):
        cols = slice(c * COL_CHUNK, (c + 1) * COL_CHUNK)
        p_ref[:, cols] = p_ref[:, cols] * _dot(a, wbp_ref[:, cols])


def _layer_spec(layer, shape, **kwargs):
    return pl.BlockSpec((None,) + shape, lambda i: (layer,) + (0,) * len(shape), **kwargs)


def _input_stage(x, g, w, bg, mix, ps, wbp, *, layer, tm, tiles_per_seq):
    t = x.shape[0]
    full = functools.partial(_layer_spec, layer)
    rows = lambda width: pl.BlockSpec((tm, width), lambda i: (i, 0))
    return pl.pallas_call(
        functools.partial(_input_stage_kernel, tiles_per_seq),
        grid=(t // tm,),
        in_specs=[rows(D_MODEL), full((1, D_MODEL)), full((D_MODEL, N_IN)), full((1, 2 * D_MODEL)),
                  full((len(POOL_WINDOWS), POOL_GROUP_DIM, POOL_GROUP_DIM)), full((1, POOL_WIDTH)),
                  full((POOL_WIDTH, D_MODEL))],
        out_specs=[rows(SB_WIDTH), rows(SB_WIDTH), rows(SB_WIDTH), rows(D_MODEL), rows(D_MODEL)],
        out_shape=[jax.ShapeDtypeStruct((t, SB_WIDTH), BF16)] * 3
                  + [jax.ShapeDtypeStruct((t, D_MODEL), F32)] * 2,
        scratch_shapes=[pltpu.VMEM((tm, D_MODEL), BF16),
                        pltpu.VMEM((tm + HALO, POOL_WIDTH), F32),
                        pltpu.VMEM((tm, POOL_WIDTH), BF16)],
        compiler_params=pltpu.CompilerParams(dimension_semantics=("arbitrary",),
                                             vmem_limit_bytes=VMEM_LIMIT_BYTES),
        name="input_stage",
    )(x, g, w, bg, mix, ps, wbp)


def _attention_schedule(lp):
    kinds = {None: 0}
    blocks = []
    for i in range(lp // Q_BLOCK):
        n = i * Q_BLOCK // K_BLOCK + 1
        for j in range(n):
            if j < n - 1:
                blocks.append((i, j * K_BLOCK, 0))
                continue
            start = min(j * K_BLOCK, lp - K_BLOCK)
            key = (j * K_BLOCK - start, i * Q_BLOCK - start)
            blocks.append((i, start, kinds.setdefault(key, len(kinds))))
    c = np.arange(K_BLOCK)[:, None]
    r = np.arange(Q_BLOCK)[None, :]
    mask = np.zeros((len(kinds), K_BLOCK, Q_BLOCK), np.float32)
    for key, kind in kinds.items():
        if key is not None:
            lo, off = key
            mask[kind] = np.where((c >= lo) & (c < r + off), 0.0, MASKED_LOGIT)
    return np.asarray(blocks, np.int32), mask


def _attention_kernel(n_blocks, sched_ref, q_ref, k_ref, v_ref, tri_ref, mask_ref, eye_ref, o_ref,
                      qs_ref, acc_ref, z_ref, sp_ref, incl_ref, w_ref):
    n_q = qs_ref.shape[0]
    lane = lax.broadcasted_iota(jnp.int32, (Q_BLOCK, LANES), 1)
    for i in range(n_q):
        q2 = q_ref[0, i * Q_BLOCK:(i + 1) * Q_BLOCK, :]
        zero = jnp.zeros_like(q2)
        qs_ref[i, 0:Q_BLOCK, :] = jnp.where(lane < SB_HEAD_DIM, q2, zero)
        qs_ref[i, Q_BLOCK:2 * Q_BLOCK, :] = jnp.where(lane >= SB_HEAD_DIM, q2, zero)
    acc_ref[...] = jnp.zeros_like(acc_ref)

    def block(t):
        return (sched_ref[t], pl.multiple_of(sched_ref[n_blocks + t], Q_BLOCK),
                sched_ref[2 * n_blocks + t])

    def raw_scores(t, slot):
        i, start, kind = block(t)
        lhs = jnp.concatenate([qs_ref[i], eye_ref[...]], axis=1)
        rhs = jnp.concatenate([k_ref[0, pl.ds(start, K_BLOCK), :], mask_ref[kind]], axis=1)
        z_ref[slot] = lax.dot_general(lhs, rhs, (((1,), (1,)), ((), ())),
                                      preferred_element_type=F32)

    def softplus(t, slot):
        z = z_ref[slot].astype(BF16)
        sp_ref[slot] = jnp.maximum(z, 0.0) + jnp.log(1.0 + jnp.exp(-jnp.abs(z)))

    def suffix_sums(t, slot):
        incl_ref[slot] = _dot(sp_ref[slot], tri_ref[...])

    def weights(t, slot):
        w_ref[slot] = jnp.exp(z_ref[slot] + incl_ref[slot]).astype(BF16)

    def accumulate(t, slot):
        i, start, _ = block(t)
        decay = jnp.exp(incl_ref[slot, :, 0:1])
        pv = _dot(w_ref[slot], v_ref[0, pl.ds(start, K_BLOCK), :])
        acc_ref[i] = acc_ref[i] * decay + pv

    stages = (raw_scores, softplus, suffix_sums, weights, accumulate)
    depth = len(stages) - 1

    def trip(t, t_mod_slots):
        for s in range(depth, -1, -1):
            if isinstance(t, int) and not 0 <= t - s < n_blocks:
                continue
            stages[s](t - s, (t_mod_slots - s) % ATTN_SLOTS)

    for t in range(depth):
        trip(t, t % ATTN_SLOTS)

    n_groups = (n_blocks - depth) // ATTN_SLOTS

    def body(g, c):
        t0 = depth + g * ATTN_SLOTS
        for u in range(ATTN_SLOTS):
            trip(t0 + u, (depth + u) % ATTN_SLOTS)
        return c

    lax.fori_loop(0, n_groups, body, 0)

    for t in range(depth + n_groups * ATTN_SLOTS, n_blocks + depth):
        trip(t, t % ATTN_SLOTS)

    for i in range(n_q):
        acc = acc_ref[i]
        o_ref[0, i * Q_BLOCK:(i + 1) * Q_BLOCK, :] = jnp.where(
            lane < SB_HEAD_DIM, acc[:Q_BLOCK], acc[Q_BLOCK:]).astype(o_ref.dtype)


def _attention(q, k, v):
    b, lp, _ = q.shape
    n_pairs = SB_WIDTH // LANES
    n_q = lp // Q_BLOCK
    blocks, mask = _attention_schedule(lp)
    n_blocks = blocks.shape[0]
    sched = jnp.asarray(blocks.T.reshape(-1))
    j = np.arange(K_BLOCK)[:, None]
    s = np.arange(K_BLOCK)[None, :]
    tri = jnp.asarray(np.where(j >= s, -1.0, 0.0), BF16)
    eye = jnp.asarray(np.tile(np.eye(Q_BLOCK), (2, 1)), BF16)
    mask = jnp.asarray(mask, BF16)
    seq = lambda: pl.BlockSpec((1, lp, LANES), lambda bi, hp, sched: (bi, 0, hp))
    return pl.pallas_call(
        functools.partial(_attention_kernel, n_blocks),
        grid_spec=pltpu.PrefetchScalarGridSpec(
            num_scalar_prefetch=1,
            grid=(b, n_pairs),
            in_specs=[seq(), seq(), seq(),
                      pl.BlockSpec(tri.shape, lambda bi, hp, sched: (0, 0)),
                      pl.BlockSpec(mask.shape, lambda bi, hp, sched: (0, 0, 0)),
                      pl.BlockSpec(eye.shape, lambda bi, hp, sched: (0, 0))],
            out_specs=seq(),
            scratch_shapes=[pltpu.VMEM((n_q, 2 * Q_BLOCK, LANES), BF16),
                            pltpu.VMEM((n_q, 2 * Q_BLOCK, LANES), F32),
                            pltpu.VMEM((ATTN_SLOTS, 2 * Q_BLOCK, K_BLOCK), F32),
                            pltpu.VMEM((ATTN_SLOTS, 2 * Q_BLOCK, K_BLOCK), BF16),
                            pltpu.VMEM((ATTN_SLOTS, 2 * Q_BLOCK, K_BLOCK), F32),
                            pltpu.VMEM((ATTN_SLOTS, 2 * Q_BLOCK, K_BLOCK), BF16)]),
        out_shape=jax.ShapeDtypeStruct((b, lp, SB_WIDTH), BF16),
        compiler_params=pltpu.CompilerParams(dimension_semantics=("arbitrary", "arbitrary")),
        name="stick_breaking_attention",
    )(sched, q, k, v, tri, mask, eye)


def _output_stage_kernel(final_norm, x_ref, s_ref, p_ref, gsb_ref, wbs_ref, wo_ref, g2_ref,
                         wfi_ref, wfo_ref, gf_ref, o_ref, m_ref, x1_ref, h2_ref, act_ref):
    s = s_ref[...]
    for c in range(D_MODEL // COL_CHUNK):
        cols = slice(c * COL_CHUNK, (c + 1) * COL_CHUNK)
        m_ref[:, cols] = (p_ref[:, cols] + gsb_ref[:, cols] * _dot(s, wbs_ref[:, cols])).astype(BF16)
    m = m_ref[...]
    for c in range(D_MODEL // COL_CHUNK):
        cols = slice(c * COL_CHUNK, (c + 1) * COL_CHUNK)
        x1_ref[:, cols] = x_ref[:, cols] + _dot(m, wo_ref[:, cols])
    h2_ref[...] = _rms_norm_rows(x1_ref[...], g2_ref[...]).astype(BF16)
    h2 = h2_ref[...]
    for c in range(D_FF // FF_CHUNK):
        cols = slice(c * FF_CHUNK, (c + 1) * FF_CHUNK)
        gt = _dot(h2, wfi_ref[:, cols])
        up = _dot(h2, wfi_ref[:, D_FF + c * FF_CHUNK:D_FF + (c + 1) * FF_CHUNK])
        act_ref[:, cols] = (gt * _sigmoid(gt) * up).astype(BF16)
    act = act_ref[...]
    for c in range(D_MODEL // COL_CHUNK):
        cols = slice(c * COL_CHUNK, (c + 1) * COL_CHUNK)
        y = x1_ref[:, cols] + _dot(act, wfo_ref[:, cols])
        if final_norm:
            x1_ref[:, cols] = y
        else:
            o_ref[:, cols] = y
    if final_norm:
        o_ref[...] = _rms_norm_rows(x1_ref[...], gf_ref[...])


def _output_stage(x, s, p, gsb, wbs, wo, g2, wfi, wfo, gf, *, layer, tm, final_norm):
    t = x.shape[0]
    full = functools.partial(_layer_spec, layer, pipeline_mode=pl.Buffered(1))
    rows = lambda width: pl.BlockSpec((tm, width), lambda i: (i, 0))
    return pl.pallas_call(
        functools.partial(_output_stage_kernel, final_norm),
        grid=(t // tm,),
        in_specs=[rows(D_MODEL), rows(SB_WIDTH), rows(D_MODEL), rows(D_MODEL),
                  full((SB_WIDTH, D_MODEL)), full((D_MODEL, D_MODEL)), full((1, D_MODEL)),
                  full((D_MODEL, 2 * D_FF)), full((D_FF, D_MODEL)),
                  pl.BlockSpec((1, D_MODEL), lambda i: (0, 0))],
        out_specs=rows(D_MODEL),
        out_shape=jax.ShapeDtypeStruct((t, D_MODEL), F32),
        scratch_shapes=[pltpu.VMEM((tm, D_MODEL), BF16),
                        pltpu.VMEM((tm, D_MODEL), F32),
                        pltpu.VMEM((tm, D_MODEL), BF16),
                        pltpu.VMEM((tm, D_FF), BF16)],
        compiler_params=pltpu.CompilerParams(dimension_semantics=("arbitrary",),
                                             vmem_limit_bytes=VMEM_LIMIT_BYTES),
        name="output_stage",
    )(x, s, p, gsb, wbs, wo, g2, wfi, wfo, gf)


def kernel(x, meta_tokens, norm1_g, w_in, b_gate, pool_mix, pool_scale, w_branch_pool, w_branch_sb,
           w_out, norm2_g, w_ffn_in, w_ffn_out, final_norm_g):
    b, seq, d = x.shape
    depth = w_in.shape[0]
    l = N_META + seq
    lp = -(-l // Q_BLOCK) * Q_BLOCK
    tiles_per_seq = 4
    tm = lp // tiles_per_seq

    meta = jnp.broadcast_to(meta_tokens[None].astype(x.dtype), (b, N_META, d))
    hcat = jnp.concatenate([meta, x, jnp.zeros((b, lp - l, d), x.dtype)], axis=1).reshape(b * lp, d)

    w_in, pool_mix, w_branch_pool, w_branch_sb, w_out, w_ffn_in, w_ffn_out = (
        w.astype(BF16) for w in (w_in, pool_mix, w_branch_pool, w_branch_sb, w_out, w_ffn_in, w_ffn_out))
    norm1_g, b_gate, pool_scale, norm2_g = (
        a[:, None, :] for a in (norm1_g, b_gate, pool_scale, norm2_g))

    for layer in range(depth):
        q, k, v, p, gsb = _input_stage(
            hcat, norm1_g, w_in, b_gate, pool_mix, pool_scale, w_branch_pool,
            layer=layer, tm=tm, tiles_per_seq=tiles_per_seq)
        s = _attention(q.reshape(b, lp, SB_WIDTH), k.reshape(b, lp, SB_WIDTH),
                       v.reshape(b, lp, SB_WIDTH)).reshape(b * lp, SB_WIDTH)
        hcat = _output_stage(
            hcat, s, p, gsb, w_branch_sb, w_out, norm2_g, w_ffn_in, w_ffn_out, final_norm_g[None],
            layer=layer, tm=tm, final_norm=(layer == depth - 1))

    return hcat.reshape(b, lp, d)[:, N_META:l]
```

```python
import functools
import math

import numpy as np

import jax
import jax.numpy as jnp
from jax import lax
from jax.experimental import pallas as pl
from jax.experimental.pallas import tpu as pltpu

D_MODEL = 1024
N_META = 16
SB_HEAD_DIM = 64
SB_WIDTH = 512
POOL_WINDOWS = (2, 4, 8, 16)
POOL_WIDTH = 512
POOL_GROUP_DIM = 128
N_IN = 4096
D_FF = 2816
RMS_EPS = 1e-6

LANES = 128
Q_BLOCK = 128
K_BLOCK = 256
ATTN_SLOTS = 8
HALO = 16
FF_CHUNK = 256
COL_CHUNK = 512
VMEM_LIMIT_BYTES = 56 * 1024 * 1024
MASKED_LOGIT = -1e30

BF16 = jnp.bfloat16
F32 = jnp.float32


def _dot(a, b):
    return jnp.dot(a, b, preferred_element_type=F32)


def _rms_norm_rows(x, g):
    ms = jnp.mean(x * x, axis=-1, keepdims=True)
    return x * lax.rsqrt(ms + RMS_EPS) * g


def _sigmoid(x):
    return 1.0 / (1.0 + jnp.exp(-x))


def _input_stage_kernel(tiles_per_seq, x_ref, g_ref, w_ref, bg_ref, mix_ref, ps_ref, wbp_ref,
                        q_ref, k_ref, v_ref, p_ref, gsb_ref,
                        h_ref, ubuf_ref, a_ref):
    tm = x_ref.shape[0]
    tile_in_seq = pl.program_id(0) % tiles_per_seq

    h_ref[...] = _rms_norm_rows(x_ref[...], g_ref[...]).astype(BF16)

    @pl.when(tile_in_seq == 0)
    def _():
        ubuf_ref[0:HALO, :] = jnp.zeros((HALO, POOL_WIDTH), F32)

    @pl.when(tile_in_seq != 0)
    def _():
        ubuf_ref[0:HALO, :] = ubuf_ref[tm:tm + HALO, :]

    h = h_ref[...]
    ubuf_ref[HALO:HALO + tm, :] = _dot(h, w_ref[:, 3 * SB_WIDTH:3 * SB_WIDTH + POOL_WIDTH])

    def gate(branch_ref, c):
        first = D_MODEL * (branch_ref is gsb_ref) + c * COL_CHUNK
        w0 = 3 * SB_WIDTH + POOL_WIDTH + first
        branch_ref[:, c * COL_CHUNK:(c + 1) * COL_CHUNK] = _sigmoid(
            _dot(h, w_ref[:, w0:w0 + COL_CHUNK]) + bg_ref[:, first:first + COL_CHUNK])

    gates = [(ref, c) for c in range(D_MODEL // COL_CHUNK) for ref in (p_ref, gsb_ref)]
    pos = tile_in_seq * tm + lax.broadcasted_iota(jnp.int32, (tm, 1), 0)
    for g, window in enumerate(POOL_WINDOWS):
        gate(*gates[g])
        cols = slice(g * POOL_GROUP_DIM, (g + 1) * POOL_GROUP_DIM)
        ug = ubuf_ref[HALO:HALO + tm, cols]
        total = ug
        for d in range(1, window):
            total = total + ubuf_ref[HALO - d:HALO - d + tm, cols]
        cnt = jnp.minimum(pos + 1, window).astype(F32)
        diff = (total / cnt - ug).astype(BF16)
        a_ref[:, cols] = (_dot(diff, mix_ref[g]) * ps_ref[:, cols]).astype(BF16)
    for extra in gates[len(POOL_WINDOWS):]:
        gate(*extra)

    q_ref[...] = (_dot(h, w_ref[:, 0:SB_WIDTH]) * (SB_HEAD_DIM ** -0.5)).astype(BF16)
    k_ref[...] = _dot(h, w_ref[:, SB_WIDTH:2 * SB_WIDTH]).astype(BF16)
    v_ref[...] = _dot(h, w_ref[:, 2 * SB_WIDTH:3 * SB_WIDTH]).astype(BF16)

    a = a_ref[...]
    for c in range(D_MODEL // COL_CHUNK):
        cols = slice(c * COL_CHUNK, (c + 1) * COL_CHUNK)
        p_ref[:, cols] = p_ref[:, cols] * _dot(a, wbp_ref[:, cols])


def _layer_spec(layer, shape, **kwargs):
    return pl.BlockSpec((None,) + shape, lambda i: (layer,) + (0,) * len(shape), **kwargs)


def _input_stage(x, g, w, bg, mix, ps, wbp, *, layer, tm, tiles_per_seq):
    t = x.shape[0]
    full = functools.partial(_layer_spec, layer)
    rows = lambda width: pl.BlockSpec((tm, width), lambda i: (i, 0))
    return pl.pallas_call(
        functools.partial(_input_stage_kernel, tiles_per_seq),
        grid=(t // tm,),
        in_specs=[rows(D_MODEL), full((1, D_MODEL)), full((D_MODEL, N_IN)), full((1, 2 * D_MODEL)),
                  full((len(POOL_WINDOWS), POOL_GROUP_DIM, POOL_GROUP_DIM)), full((1, POOL_WIDTH)),
                  full((POOL_WIDTH, D_MODEL))],
        out_specs=[rows(SB_WIDTH), rows(SB_WIDTH), rows(SB_WIDTH), rows(D_MODEL), rows(D_MODEL)],
        out_shape=[jax.ShapeDtypeStruct((t, SB_WIDTH), BF16)] * 3
                  + [jax.ShapeDtypeStruct((t, D_MODEL), F32)] * 2,
        scratch_shapes=[pltpu.VMEM((tm, D_MODEL), BF16),
                        pltpu.VMEM((tm + HALO, POOL_WIDTH), F32),
                        pltpu.VMEM((tm, POOL_WIDTH), BF16)],
        compiler_params=pltpu.CompilerParams(dimension_semantics=("arbitrary",),
                                             vmem_limit_bytes=VMEM_LIMIT_BYTES),
        name="input_stage",
    )(x, g, w, bg, mix, ps, wbp)


def _attention_schedule(lp):
    kinds = {None: 0}
    blocks = []
    for i in range(lp // Q_BLOCK):
        n = i * Q_BLOCK // K_BLOCK + 1
        for j in range(n):
            if j < n - 1:
                blocks.append((i, j * K_BLOCK, 0))
                continue
            start = min(j * K_BLOCK, lp - K_BLOCK)
            key = (j * K_BLOCK - start, i * Q_BLOCK - start)
            blocks.append((i, start, kinds.setdefault(key, len(kinds))))
    c = np.arange(K_BLOCK)[:, None]
    r = np.arange(Q_BLOCK)[None, :]
    mask = np.zeros((len(kinds), K_BLOCK, Q_BLOCK), np.float32)
    for key, kind in kinds.items():
        if key is not None:
            lo, off = key
            mask[kind] = np.where((c >= lo) & (c < r + off), 0.0, MASKED_LOGIT)
    return np.asarray(blocks, np.int32), mask


def _attention_kernel(n_blocks, sched_ref, q_ref, k_ref, v_ref, tri_ref, mask_ref, eye_ref, o_ref,
                      qs_ref, vt_ref, acc_ref, z_ref, sp_ref, incl_ref, w_ref):
    n_q = qs_ref.shape[0]
    lane = lax.broadcasted_iota(jnp.int32, (Q_BLOCK, LANES), 1)
    for i in range(n_q):
        rows = slice(i * Q_BLOCK, (i + 1) * Q_BLOCK)
        q2 = q_ref[0, rows, :]
        zero = jnp.zeros_like(q2)
        qs_ref[i, 0:LANES, 0:Q_BLOCK] = jnp.where(lane < SB_HEAD_DIM, q2, zero).T
        qs_ref[i, 0:LANES, Q_BLOCK:2 * Q_BLOCK] = jnp.where(lane >= SB_HEAD_DIM, q2, zero).T
        qs_ref[i, LANES:LANES + Q_BLOCK, :] = eye_ref[...]
        vt_ref[i] = v_ref[0, rows, :].T
    acc_ref[...] = jnp.zeros_like(acc_ref)

    def block(t):
        return (sched_ref[t], pl.multiple_of(sched_ref[n_blocks + t], Q_BLOCK),
                sched_ref[2 * n_blocks + t])

    def raw_scores(t, slot):
        i, start, kind = block(t)
        keys = jnp.concatenate([k_ref[0, pl.ds(start, K_BLOCK), :], mask_ref[kind]], axis=1)
        z_ref[slot] = _dot(keys, qs_ref[i])

    def softplus(t, slot):
        z = z_ref[slot].astype(BF16)
        sp_ref[slot] = jnp.maximum(z, 0.0) + jnp.log(1.0 + jnp.exp(-jnp.abs(z)))

    def suffix_sums(t, slot):
        incl_ref[slot] = _dot(tri_ref[...], sp_ref[slot])

    def weights(t, slot):
        w_ref[slot] = jnp.exp(z_ref[slot] + incl_ref[slot]).astype(BF16)

    def accumulate(t, slot):
        i, start, _ = block(t)
        c = lax.shift_right_logical(start, int(math.log2(Q_BLOCK)))
        vt = jnp.concatenate([vt_ref[c], vt_ref[c + 1]], axis=1)
        decay = jnp.exp(incl_ref[slot, 0:1, :])
        acc_ref[i] = acc_ref[i] * decay + _dot(vt, w_ref[slot])

    stages = (raw_scores, softplus, suffix_sums, weights, accumulate)
    depth = len(stages) - 1

    def trip(t, t_mod_slots):
        for s in range(depth, -1, -1):
            if isinstance(t, int) and not 0 <= t - s < n_blocks:
                continue
            stages[s](t - s, (t_mod_slots - s) % ATTN_SLOTS)

    for t in range(depth):
        trip(t, t % ATTN_SLOTS)

    n_groups = (n_blocks - depth) // ATTN_SLOTS

    def body(g, c):
        t0 = depth + g * ATTN_SLOTS
        for u in range(ATTN_SLOTS):
            trip(t0 + u, (depth + u) % ATTN_SLOTS)
        return c

    lax.fori_loop(0, n_groups, body, 0)

    for t in range(depth + n_groups * ATTN_SLOTS, n_blocks + depth):
        trip(t, t % ATTN_SLOTS)

    for i in range(n_q):
        acc = acc_ref[i]
        both = jnp.concatenate([acc[0:SB_HEAD_DIM, 0:Q_BLOCK],
                                acc[SB_HEAD_DIM:2 * SB_HEAD_DIM, Q_BLOCK:2 * Q_BLOCK]], axis=0)
        o_ref[0, i * Q_BLOCK:(i + 1) * Q_BLOCK, :] = both.T.astype(o_ref.dtype)


def _attention(q, k, v):
    b, lp, _ = q.shape
    n_pairs = SB_WIDTH // LANES
    n_q = lp // Q_BLOCK
    blocks, mask = _attention_schedule(lp)
    n_blocks = blocks.shape[0]
    sched = jnp.asarray(blocks.T.reshape(-1))
    j = np.arange(K_BLOCK)[:, None]
    s = np.arange(K_BLOCK)[None, :]
    tri = jnp.asarray(np.where(j >= s, -1.0, 0.0).T, BF16)
    eye = jnp.asarray(np.tile(np.eye(Q_BLOCK), (1, 2)), BF16)
    mask = jnp.asarray(mask, BF16)
    seq = lambda: pl.BlockSpec((1, lp, LANES), lambda bi, hp, sched: (bi, 0, hp))
    return pl.pallas_call(
        functools.partial(_attention_kernel, n_blocks),
        grid_spec=pltpu.PrefetchScalarGridSpec(
            num_scalar_prefetch=1,
            grid=(b, n_pairs),
            in_specs=[seq(), seq(), seq(),
                      pl.BlockSpec(tri.shape, lambda bi, hp, sched: (0, 0)),
                      pl.BlockSpec(mask.shape, lambda bi, hp, sched: (0, 0, 0)),
                      pl.BlockSpec(eye.shape, lambda bi, hp, sched: (0, 0))],
            out_specs=seq(),
            scratch_shapes=[pltpu.VMEM((n_q, LANES + Q_BLOCK, 2 * Q_BLOCK), BF16),
                            pltpu.VMEM((n_q, LANES, Q_BLOCK), BF16),
                            pltpu.VMEM((n_q, LANES, 2 * Q_BLOCK), F32),
                            pltpu.VMEM((ATTN_SLOTS, 2 * Q_BLOCK, K_BLOCK), F32),
                            pltpu.VMEM((ATTN_SLOTS, 2 * Q_BLOCK, K_BLOCK), BF16),
                            pltpu.VMEM((ATTN_SLOTS, 2 * Q_BLOCK, K_BLOCK), F32),
                            pltpu.VMEM((ATTN_SLOTS, 2 * Q_BLOCK, K_BLOCK), BF16)]),
        out_shape=jax.ShapeDtypeStruct((b, lp, SB_WIDTH), BF16),
        compiler_params=pltpu.CompilerParams(dimension_semantics=("arbitrary", "arbitrary")),
        name="stick_breaking_attention",
    )(sched, q, k, v, tri, mask, eye)


def _output_stage_kernel(final_norm, x_ref, s_ref, p_ref, gsb_ref, wbs_ref, wo_ref, g2_ref,
                         wfi_ref, wfo_ref, gf_ref, o_ref, m_ref, x1_ref, h2_ref, act_ref):
    s = s_ref[...]
    for c in range(D_MODEL // COL_CHUNK):
        cols = slice(c * COL_CHUNK, (c + 1) * COL_CHUNK)
        m_ref[:, cols] = (p_ref[:, cols] + gsb_ref[:, cols] * _dot(s, wbs_ref[:, cols])).astype(BF16)
    m = m_ref[...]
    for c in range(D_MODEL // COL_CHUNK):
        cols = slice(c * COL_CHUNK, (c + 1) * COL_CHUNK)
        x1_ref[:, cols] = x_ref[:, cols] + _dot(m, wo_ref[:, cols])
    h2_ref[...] = _rms_norm_rows(x1_ref[...], g2_ref[...]).astype(BF16)
    h2 = h2_ref[...]
    for c in range(D_FF // FF_CHUNK):
        cols = slice(c * FF_CHUNK, (c + 1) * FF_CHUNK)
        gt = _dot(h2, wfi_ref[:, cols])
        up = _dot(h2, wfi_ref[:, D_FF + c * FF_CHUNK:D_FF + (c + 1) * FF_CHUNK])
        act_ref[:, cols] = (gt * _sigmoid(gt) * up).astype(BF16)
    act = act_ref[...]
    for c in range(D_MODEL // COL_CHUNK):
        cols = slice(c * COL_CHUNK, (c + 1) * COL_CHUNK)
        y = x1_ref[:, cols] + _dot(act, wfo_ref[:, cols])
        if final_norm:
            x1_ref[:, cols] = y
        else:
            o_ref[:, cols] = y
    if final_norm:
        o_ref[...] = _rms_norm_rows(x1_ref[...], gf_ref[...])


def _output_stage(x, s, p, gsb, wbs, wo, g2, wfi, wfo, gf, *, layer, tm, final_norm):
    t = x.shape[0]
    full = functools.partial(_layer_spec, layer, pipeline_mode=pl.Buffered(1))
    rows = lambda width: pl.BlockSpec((tm, width), lambda i: (i, 0))
    return pl.pallas_call(
        functools.partial(_output_stage_kernel, final_norm),
        grid=(t // tm,),
        in_specs=[rows(D_MODEL), rows(SB_WIDTH), rows(D_MODEL), rows(D_MODEL),
                  full((SB_WIDTH, D_MODEL)), full((D_MODEL, D_MODEL)), full((1, D_MODEL)),
                  full((D_MODEL, 2 * D_FF)), full((D_FF, D_MODEL)),
                  pl.BlockSpec((1, D_MODEL), lambda i: (0, 0))],
        out_specs=rows(D_MODEL),
        out_shape=jax.ShapeDtypeStruct((t, D_MODEL), F32),
        scratch_shapes=[pltpu.VMEM((tm, D_MODEL), BF16),
                        pltpu.VMEM((tm, D_MODEL), F32),
                        pltpu.VMEM((tm, D_MODEL), BF16),
                        pltpu.VMEM((tm, D_FF), BF16)],
        compiler_params=pltpu.CompilerParams(dimension_semantics=("arbitrary",),
                                             vmem_limit_bytes=VMEM_LIMIT_BYTES),
        name="output_stage",
    )(x, s, p, gsb, wbs, wo, g2, wfi, wfo, gf)


def kernel(x, meta_tokens, norm1_g, w_in, b_gate, pool_mix, pool_scale, w_branch_pool, w_branch_sb,
           w_out, norm2_g, w_ffn_in, w_ffn_out, final_norm_g):
    b, seq, d = x.shape
    depth = w_in.shape[0]
    l = N_META + seq
    lp = -(-l // Q_BLOCK) * Q_BLOCK
    tiles_per_seq = 4
    tm = lp // tiles_per_seq

    meta = jnp.broadcast_to(meta_tokens[None].astype(x.dtype), (b, N_META, d))
    hcat = jnp.concatenate([meta, x, jnp.zeros((b, lp - l, d), x.dtype)], axis=1).reshape(b * lp, d)

    w_in, pool_mix, w_branch_pool, w_branch_sb, w_out, w_ffn_in, w_ffn_out = (
        w.astype(BF16) for w in (w_in, pool_mix, w_branch_pool, w_branch_sb, w_out, w_ffn_in, w_ffn_out))
    norm1_g, b_gate, pool_scale, norm2_g = (
        a[:, None, :] for a in (norm1_g, b_gate, pool_scale, norm2_g))

    for layer in range(depth):
        q, k, v, p, gsb = _input_stage(
            hcat, norm1_g, w_in, b_gate, pool_mix, pool_scale, w_branch_pool,
            layer=layer, tm=tm, tiles_per_seq=tiles_per_seq)
        s = _attention(q.reshape(b, lp, SB_WIDTH), k.reshape(b, lp, SB_WIDTH),
                       v.reshape(b, lp, SB_WIDTH)).reshape(b * lp, SB_WIDTH)
        hcat = _output_stage(
            hcat, s, p, gsb, w_branch_sb, w_out, norm2_g, w_ffn_in, w_ffn_out, final_norm_g[None],
            layer=layer, tm=tm, final_norm=(layer == depth - 1))

    return hcat.reshape(b, lp, d)[:, N_META:l]
```

```python
import functools

import numpy as np

import jax
import jax.numpy as jnp
from jax import lax
from jax.experimental import pallas as pl
from jax.experimental.pallas import tpu as pltpu

D_MODEL = 1024
N_META = 16
SB_HEAD_DIM = 64
SB_WIDTH = 512
POOL_WINDOWS = (2, 4, 8, 16)
POOL_WIDTH = 512
POOL_GROUP_DIM = 128
N_IN = 4096
D_FF = 2816
RMS_EPS = 1e-6

LANES = 128
BF16_ROWS = 16
Q_BLOCK = 128
K_BLOCK = 256
ATTN_SLOTS = 8
HALO = 16
FF_CHUNK = 256
COL_CHUNK = 512
VMEM_LIMIT_BYTES = 56 * 1024 * 1024
MASKED_LOGIT = -1e30

BF16 = jnp.bfloat16
F32 = jnp.float32


def _dot(a, b):
    return jnp.dot(a, b, preferred_element_type=F32)


def _rms_norm_rows(x, g):
    ms = jnp.mean(x * x, axis=-1, keepdims=True)
    return x * lax.rsqrt(ms + RMS_EPS) * g


def _sigmoid(x):
    return 1.0 / (1.0 + jnp.exp(-x))


def _input_stage_kernel(tiles_per_seq, x_ref, g_ref, w_ref, bg_ref, mix_ref, ps_ref, wbp_ref,
                        q_ref, k_ref, v_ref, p_ref, gsb_ref,
                        h_ref, ubuf_ref, a_ref):
    tm = x_ref.shape[0]
    tile_in_seq = pl.program_id(0) % tiles_per_seq

    h_ref[...] = _rms_norm_rows(x_ref[...], g_ref[...]).astype(BF16)

    @pl.when(tile_in_seq == 0)
    def _():
        ubuf_ref[0:HALO, :] = jnp.zeros((HALO, POOL_WIDTH), F32)

    @pl.when(tile_in_seq != 0)
    def _():
        ubuf_ref[0:HALO, :] = ubuf_ref[tm:tm + HALO, :]

    h = h_ref[...]
    ubuf_ref[HALO:HALO + tm, :] = _dot(h, w_ref[:, 3 * SB_WIDTH:3 * SB_WIDTH + POOL_WIDTH])

    def gate(branch_ref, c):
        first = D_MODEL * (branch_ref is gsb_ref) + c * COL_CHUNK
        w0 = 3 * SB_WIDTH + POOL_WIDTH + first
        branch_ref[:, c * COL_CHUNK:(c + 1) * COL_CHUNK] = _sigmoid(
            _dot(h, w_ref[:, w0:w0 + COL_CHUNK]) + bg_ref[:, first:first + COL_CHUNK])

    gates = [(ref, c) for c in range(D_MODEL // COL_CHUNK) for ref in (p_ref, gsb_ref)]
    pos = tile_in_seq * tm + lax.broadcasted_iota(jnp.int32, (tm, 1), 0)
    for g, window in enumerate(POOL_WINDOWS):
        gate(*gates[g])
        cols = slice(g * POOL_GROUP_DIM, (g + 1) * POOL_GROUP_DIM)
        ug = ubuf_ref[HALO:HALO + tm, cols]
        total = ug
        for d in range(1, window):
            total = total + ubuf_ref[HALO - d:HALO - d + tm, cols]
        cnt = jnp.minimum(pos + 1, window).astype(F32)
        diff = (total / cnt - ug).astype(BF16)
        a_ref[:, cols] = (_dot(diff, mix_ref[g]) * ps_ref[:, cols]).astype(BF16)
    for extra in gates[len(POOL_WINDOWS):]:
        gate(*extra)

    q_ref[...] = (_dot(h, w_ref[:, 0:SB_WIDTH]) * (SB_HEAD_DIM ** -0.5)).astype(BF16)
    k_ref[...] = _dot(h, w_ref[:, SB_WIDTH:2 * SB_WIDTH]).astype(BF16)
    v_ref[...] = _dot(h, w_ref[:, 2 * SB_WIDTH:3 * SB_WIDTH]).astype(BF16)

    a = a_ref[...]
    for c in range(D_MODEL // COL_CHUNK):
        cols = slice(c * COL_CHUNK, (c + 1) * COL_CHUNK)
        p_ref[:, cols] = p_ref[:, cols] * _dot(a, wbp_ref[:, cols])


def _layer_spec(layer, shape, **kwargs):
    return pl.BlockSpec((None,) + shape, lambda i: (layer,) + (0,) * len(shape), **kwargs)


def _input_stage(x, g, w, bg, mix, ps, wbp, *, layer, tm, tiles_per_seq):
    t = x.shape[0]
    full = functools.partial(_layer_spec, layer)
    rows = lambda width: pl.BlockSpec((tm, width), lambda i: (i, 0))
    return pl.pallas_call(
        functools.partial(_input_stage_kernel, tiles_per_seq),
        grid=(t // tm,),
        in_specs=[rows(D_MODEL), full((1, D_MODEL)), full((D_MODEL, N_IN)), full((1, 2 * D_MODEL)),
                  full((len(POOL_WINDOWS), POOL_GROUP_DIM, POOL_GROUP_DIM)), full((1, POOL_WIDTH)),
                  full((POOL_WIDTH, D_MODEL))],
        out_specs=[rows(SB_WIDTH), rows(SB_WIDTH), rows(SB_WIDTH), rows(D_MODEL), rows(D_MODEL)],
        out_shape=[jax.ShapeDtypeStruct((t, SB_WIDTH), BF16)] * 3
                  + [jax.ShapeDtypeStruct((t, D_MODEL), F32)] * 2,
        scratch_shapes=[pltpu.VMEM((tm, D_MODEL), BF16),
                        pltpu.VMEM((tm + HALO, POOL_WIDTH), F32),
                        pltpu.VMEM((tm, POOL_WIDTH), BF16)],
        compiler_params=pltpu.CompilerParams(dimension_semantics=("arbitrary",),
                                             vmem_limit_bytes=VMEM_LIMIT_BYTES),
        name="input_stage",
    )(x, g, w, bg, mix, ps, wbp)


def _attention_schedule(lp):
    kinds = {None: 0}
    blocks = []
    for i in range(-(-lp // Q_BLOCK)):
        n = i * Q_BLOCK // K_BLOCK + 1
        for j in range(n):
            if j < n - 1:
                blocks.append((i, j * K_BLOCK, 0))
                continue
            start = min(j * K_BLOCK, lp - K_BLOCK)
            key = (j * K_BLOCK - start, i * Q_BLOCK - start)
            blocks.append((i, start, kinds.setdefault(key, len(kinds))))
    c = np.arange(K_BLOCK)[:, None]
    r = np.arange(Q_BLOCK)[None, :]
    mask = np.zeros((len(kinds), K_BLOCK, Q_BLOCK), np.float32)
    for key, kind in kinds.items():
        if key is not None:
            lo, off = key
            mask[kind] = np.where((c >= lo) & (c < r + off), 0.0, MASKED_LOGIT)
    return np.asarray(blocks, np.int32), mask


def _attention_kernel(n_blocks, sched_ref, q_ref, k_ref, v_ref, tri_ref, mask_ref, eye_ref, o_ref,
                      qs_ref, acc_ref, z_ref, sp_ref, incl_ref, w_ref):
    n_q = qs_ref.shape[0]
    seq_len = q_ref.shape[1]
    lane = lax.broadcasted_iota(jnp.int32, (Q_BLOCK, LANES), 1)
    for i in range(n_q):
        n_rows = min(Q_BLOCK, seq_len - i * Q_BLOCK)
        q2 = q_ref[0, i * Q_BLOCK:i * Q_BLOCK + n_rows, :]
        if n_rows < Q_BLOCK:
            q2 = jnp.concatenate([q2, jnp.zeros((Q_BLOCK - n_rows, LANES), q2.dtype)], axis=0)
        zero = jnp.zeros_like(q2)
        qs_ref[i, 0:Q_BLOCK, :] = jnp.where(lane < SB_HEAD_DIM, q2, zero)
        qs_ref[i, Q_BLOCK:2 * Q_BLOCK, :] = jnp.where(lane >= SB_HEAD_DIM, q2, zero)
    acc_ref[...] = jnp.zeros_like(acc_ref)

    def block(t):
        return (sched_ref[t], pl.multiple_of(sched_ref[n_blocks + t], BF16_ROWS),
                sched_ref[2 * n_blocks + t])

    def raw_scores(t, slot):
        i, start, kind = block(t)
        lhs = jnp.concatenate([qs_ref[i], eye_ref[...]], axis=1)
        rhs = jnp.concatenate([k_ref[0, pl.ds(start, K_BLOCK), :], mask_ref[kind]], axis=1)
        z_ref[slot] = lax.dot_general(lhs, rhs, (((1,), (1,)), ((), ())),
                                      preferred_element_type=F32)

    def softplus(t, slot):
        z = z_ref[slot].astype(BF16)
        sp_ref[slot] = jnp.maximum(z, 0.0) + jnp.log(1.0 + jnp.exp(-jnp.abs(z)))

    def suffix_sums(t, slot):
        incl_ref[slot] = _dot(sp_ref[slot], tri_ref[...])

    def weights(t, slot):
        w_ref[slot] = jnp.exp(z_ref[slot] + incl_ref[slot]).astype(BF16)

    def accumulate(t, slot):
        i, start, _ = block(t)
        decay = jnp.exp(incl_ref[slot, :, 0:1])
        pv = _dot(w_ref[slot], v_ref[0, pl.ds(start, K_BLOCK), :])
        acc_ref[i] = acc_ref[i] * decay + pv

    stages = (raw_scores, softplus, suffix_sums, weights, accumulate)
    depth = len(stages) - 1

    def trip(t, t_mod_slots):
        for s in range(depth, -1, -1):
            if isinstance(t, int) and not 0 <= t - s < n_blocks:
                continue
            stages[s](t - s, (t_mod_slots - s) % ATTN_SLOTS)

    for t in range(depth):
        trip(t, t % ATTN_SLOTS)

    n_groups = (n_blocks - depth) // ATTN_SLOTS

    def body(g, c):
        t0 = depth + g * ATTN_SLOTS
        for u in range(ATTN_SLOTS):
            trip(t0 + u, (depth + u) % ATTN_SLOTS)
        return c

    lax.fori_loop(0, n_groups, body, 0)

    for t in range(depth + n_groups * ATTN_SLOTS, n_blocks + depth):
        trip(t, t % ATTN_SLOTS)

    for i in range(n_q):
        n_rows = min(Q_BLOCK, seq_len - i * Q_BLOCK)
        acc = acc_ref[i]
        both = jnp.where(lane < SB_HEAD_DIM, acc[:Q_BLOCK], acc[Q_BLOCK:]).astype(o_ref.dtype)
        o_ref[0, i * Q_BLOCK:i * Q_BLOCK + n_rows, :] = both[:n_rows]


def _attention(q, k, v):
    b, lp, _ = q.shape
    n_pairs = SB_WIDTH // LANES
    assert lp % BF16_ROWS == 0 and lp >= K_BLOCK, lp
    n_q = -(-lp // Q_BLOCK)
    blocks, mask = _attention_schedule(lp)
    n_blocks = blocks.shape[0]
    sched = jnp.asarray(blocks.T.reshape(-1))
    j = np.arange(K_BLOCK)[:, None]
    s = np.arange(K_BLOCK)[None, :]
    tri = jnp.asarray(np.where(j >= s, -1.0, 0.0), BF16)
    eye = jnp.asarray(np.tile(np.eye(Q_BLOCK), (2, 1)), BF16)
    mask = jnp.asarray(mask, BF16)
    seq = lambda: pl.BlockSpec((1, lp, LANES), lambda bi, hp, sched: (bi, 0, hp))
    return pl.pallas_call(
        functools.partial(_attention_kernel, n_blocks),
        grid_spec=pltpu.PrefetchScalarGridSpec(
            num_scalar_prefetch=1,
            grid=(b, n_pairs),
            in_specs=[seq(), seq(), seq(),
                      pl.BlockSpec(tri.shape, lambda bi, hp, sched: (0, 0)),
                      pl.BlockSpec(mask.shape, lambda bi, hp, sched: (0, 0, 0)),
                      pl.BlockSpec(eye.shape, lambda bi, hp, sched: (0, 0))],
            out_specs=seq(),
            scratch_shapes=[pltpu.VMEM((n_q, 2 * Q_BLOCK, LANES), BF16),
                            pltpu.VMEM((n_q, 2 * Q_BLOCK, LANES), F32),
                            pltpu.VMEM((ATTN_SLOTS, 2 * Q_BLOCK, K_BLOCK), F32),
                            pltpu.VMEM((ATTN_SLOTS, 2 * Q_BLOCK, K_BLOCK), BF16),
                            pltpu.VMEM((ATTN_SLOTS, 2 * Q_BLOCK, K_BLOCK), F32),
                            pltpu.VMEM((ATTN_SLOTS, 2 * Q_BLOCK, K_BLOCK), BF16)]),
        out_shape=jax.ShapeDtypeStruct((b, lp, SB_WIDTH), BF16),
        compiler_params=pltpu.CompilerParams(dimension_semantics=("arbitrary", "arbitrary")),
        name="stick_breaking_attention",
    )(sched, q, k, v, tri, mask, eye)


def _output_stage_kernel(final_norm, x_ref, s_ref, p_ref, gsb_ref, wbs_ref, wo_ref, g2_ref,
                         wfi_ref, wfo_ref, gf_ref, o_ref, m_ref, x1_ref, h2_ref, act_ref):
    s = s_ref[...]
    for c in range(D_MODEL // COL_CHUNK):
        cols = slice(c * COL_CHUNK, (c + 1) * COL_CHUNK)
        m_ref[:, cols] = (p_ref[:, cols] + gsb_ref[:, cols] * _dot(s, wbs_ref[:, cols])).astype(BF16)
    m = m_ref[...]
    for c in range(D_MODEL // COL_CHUNK):
        cols = slice(c * COL_CHUNK, (c + 1) * COL_CHUNK)
        x1_ref[:, cols] = x_ref[:, cols] + _dot(m, wo_ref[:, cols])
    h2_ref[...] = _rms_norm_rows(x1_ref[...], g2_ref[...]).astype(BF16)
    h2 = h2_ref[...]
    for c in range(D_FF // FF_CHUNK):
        cols = slice(c * FF_CHUNK, (c + 1) * FF_CHUNK)
        gt = _dot(h2, wfi_ref[:, cols])
        up = _dot(h2, wfi_ref[:, D_FF + c * FF_CHUNK:D_FF + (c + 1) * FF_CHUNK])
        act_ref[:, cols] = (gt * _sigmoid(gt) * up).astype(BF16)
    act = act_ref[...]
    for c in range(D_MODEL // COL_CHUNK):
        cols = slice(c * COL_CHUNK, (c + 1) * COL_CHUNK)
        y = x1_ref[:, cols] + _dot(act, wfo_ref[:, cols])
        if final_norm:
            x1_ref[:, cols] = y
        else:
            o_ref[:, cols] = y
    if final_norm:
        o_ref[...] = _rms_norm_rows(x1_ref[...], gf_ref[...])


def _output_stage(x, s, p, gsb, wbs, wo, g2, wfi, wfo, gf, *, layer, tm, final_norm):
    t = x.shape[0]
    full = functools.partial(_layer_spec, layer, pipeline_mode=pl.Buffered(1))
    rows = lambda width: pl.BlockSpec((tm, width), lambda i: (i, 0))
    return pl.pallas_call(
        functools.partial(_output_stage_kernel, final_norm),
        grid=(t // tm,),
        in_specs=[rows(D_MODEL), rows(SB_WIDTH), rows(D_MODEL), rows(D_MODEL),
                  full((SB_WIDTH, D_MODEL)), full((D_MODEL, D_MODEL)), full((1, D_MODEL)),
                  full((D_MODEL, 2 * D_FF)), full((D_FF, D_MODEL)),
                  pl.BlockSpec((1, D_MODEL), lambda i: (0, 0))],
        out_specs=rows(D_MODEL),
        out_shape=jax.ShapeDtypeStruct((t, D_MODEL), F32),
        scratch_shapes=[pltpu.VMEM((tm, D_MODEL), BF16),
                        pltpu.VMEM((tm, D_MODEL), F32),
                        pltpu.VMEM((tm, D_MODEL), BF16),
                        pltpu.VMEM((tm, D_FF), BF16)],
        compiler_params=pltpu.CompilerParams(dimension_semantics=("arbitrary",),
                                             vmem_limit_bytes=VMEM_LIMIT_BYTES),
        name="output_stage",
    )(x, s, p, gsb, wbs, wo, g2, wfi, wfo, gf)


def kernel(x, meta_tokens, norm1_g, w_in, b_gate, pool_mix, pool_scale, w_branch_pool, w_branch_sb,
           w_out, norm2_g, w_ffn_in, w_ffn_out, final_norm_g):
    b, seq, d = x.shape
    depth = w_in.shape[0]
    lp = N_META + seq
    tiles_per_seq = 3
    tm = lp // tiles_per_seq
    assert tm * tiles_per_seq == lp and tm % BF16_ROWS == 0 and tm >= HALO, (lp, tm)

    meta = jnp.broadcast_to(meta_tokens[None].astype(x.dtype), (b, N_META, d))
    hcat = jnp.concatenate([meta, x], axis=1).reshape(b * lp, d)

    w_in, pool_mix, w_branch_pool, w_branch_sb, w_out, w_ffn_in, w_ffn_out = (
        w.astype(BF16) for w in (w_in, pool_mix, w_branch_pool, w_branch_sb, w_out, w_ffn_in, w_ffn_out))
    norm1_g, b_gate, pool_scale, norm2_g = (
        a[:, None, :] for a in (norm1_g, b_gate, pool_scale, norm2_g))

    for layer in range(depth):
        q, k, v, p, gsb = _input_stage(
            hcat, norm1_g, w_in, b_gate, pool_mix, pool_scale, w_branch_pool,
            layer=layer, tm=tm, tiles_per_seq=tiles_per_seq)
        s = _attention(q.reshape(b, lp, SB_WIDTH), k.reshape(b, lp, SB_WIDTH),
                       v.reshape(b, lp, SB_WIDTH)).reshape(b * lp, SB_WIDTH)
        hcat = _output_stage(
            hcat, s, p, gsb, w_branch_sb, w_out, norm2_g, w_ffn_in, w_ffn_out, final_norm_g[None],
            layer=layer, tm=tm, final_norm=(layer == depth - 1))

    return hcat.reshape(b, lp, d)[:, N_META:]
```

```python
import functools

import numpy as np

import jax
import jax.numpy as jnp
from jax import lax
from jax.experimental import pallas as pl
from jax.experimental.pallas import tpu as pltpu

D_MODEL = 1024
N_META = 16
SB_HEAD_DIM = 64
SB_WIDTH = 512
POOL_WINDOWS = (2, 4, 8, 16)
POOL_WIDTH = 512
POOL_GROUP_DIM = 128
N_IN = 4096
D_FF = 2816
RMS_EPS = 1e-6

LANES = 128
BF16_ROWS = 16
Q_BLOCK = 128
K_BLOCK = 256
ATTN_SLOTS = 8
HALO = 16
FINAL_TILE = 512
FF_CHUNK = 256
COL_CHUNK = 512
VMEM_LIMIT_BYTES = 56 * 1024 * 1024
MASKED_LOGIT = -1e30

BF16 = jnp.bfloat16
F32 = jnp.float32


def _dot(a, b):
    return jnp.dot(a, b, preferred_element_type=F32)


def _rms_norm_rows(x, g):
    ms = jnp.mean(x * x, axis=-1, keepdims=True)
    return x * lax.rsqrt(ms + RMS_EPS) * g


def _sigmoid(x):
    return 1.0 / (1.0 + jnp.exp(-x))


def _input_stage_kernel(tiles_per_seq, x_ref, g_ref, w_ref, bg_ref, mix_ref, ps_ref, wbp_ref,
                        q_ref, k_ref, v_ref, p_ref, gsb_ref,
                        h_ref, ubuf_ref, a_ref):
    tm = x_ref.shape[0]
    tile_in_seq = pl.program_id(0) % tiles_per_seq

    h_ref[...] = _rms_norm_rows(x_ref[...], g_ref[...]).astype(BF16)

    @pl.when(tile_in_seq == 0)
    def _():
        ubuf_ref[0:HALO, :] = jnp.zeros((HALO, POOL_WIDTH), F32)

    @pl.when(tile_in_seq != 0)
    def _():
        ubuf_ref[0:HALO, :] = ubuf_ref[tm:tm + HALO, :]

    h = h_ref[...]
    ubuf_ref[HALO:HALO + tm, :] = _dot(h, w_ref[:, 3 * SB_WIDTH:3 * SB_WIDTH + POOL_WIDTH])

    def gate(branch_ref, c):
        first = D_MODEL * (branch_ref is gsb_ref) + c * COL_CHUNK
        w0 = 3 * SB_WIDTH + POOL_WIDTH + first
        branch_ref[:, c * COL_CHUNK:(c + 1) * COL_CHUNK] = _sigmoid(
            _dot(h, w_ref[:, w0:w0 + COL_CHUNK]) + bg_ref[:, first:first + COL_CHUNK])

    gates = [(ref, c) for c in range(D_MODEL // COL_CHUNK) for ref in (p_ref, gsb_ref)]
    pos = tile_in_seq * tm + lax.broadcasted_iota(jnp.int32, (tm, 1), 0)
    for g, window in enumerate(POOL_WINDOWS):
        gate(*gates[g])
        cols = slice(g * POOL_GROUP_DIM, (g + 1) * POOL_GROUP_DIM)
        ug = ubuf_ref[HALO:HALO + tm, cols]
        total = ug
        for d in range(1, window):
            total = total + ubuf_ref[HALO - d:HALO - d + tm, cols]
        cnt = jnp.minimum(pos + 1, window).astype(F32)
        diff = (total / cnt - ug).astype(BF16)
        a_ref[:, cols] = (_dot(diff, mix_ref[g]) * ps_ref[:, cols]).astype(BF16)
    for extra in gates[len(POOL_WINDOWS):]:
        gate(*extra)

    q_ref[...] = (_dot(h, w_ref[:, 0:SB_WIDTH]) * (SB_HEAD_DIM ** -0.5)).astype(BF16)
    k_ref[...] = _dot(h, w_ref[:, SB_WIDTH:2 * SB_WIDTH]).astype(BF16)
    v_ref[...] = _dot(h, w_ref[:, 2 * SB_WIDTH:3 * SB_WIDTH]).astype(BF16)

    a = a_ref[...]
    for c in range(D_MODEL // COL_CHUNK):
        cols = slice(c * COL_CHUNK, (c + 1) * COL_CHUNK)
        p_ref[:, cols] = p_ref[:, cols] * _dot(a, wbp_ref[:, cols])


def _layer_spec(layer, shape, **kwargs):
    return pl.BlockSpec((None,) + shape, lambda *_: (layer,) + (0,) * len(shape), **kwargs)


def _input_stage(x, g, w, bg, mix, ps, wbp, *, layer, tm, tiles_per_seq):
    t = x.shape[0]
    full = functools.partial(_layer_spec, layer)
    rows = lambda width: pl.BlockSpec((tm, width), lambda i: (i, 0))
    return pl.pallas_call(
        functools.partial(_input_stage_kernel, tiles_per_seq),
        grid=(t // tm,),
        in_specs=[rows(D_MODEL), full((1, D_MODEL)), full((D_MODEL, N_IN)), full((1, 2 * D_MODEL)),
                  full((len(POOL_WINDOWS), POOL_GROUP_DIM, POOL_GROUP_DIM)), full((1, POOL_WIDTH)),
                  full((POOL_WIDTH, D_MODEL))],
        out_specs=[rows(SB_WIDTH), rows(SB_WIDTH), rows(SB_WIDTH), rows(D_MODEL), rows(D_MODEL)],
        out_shape=[jax.ShapeDtypeStruct((t, SB_WIDTH), BF16)] * 3
                  + [jax.ShapeDtypeStruct((t, D_MODEL), F32)] * 2,
        scratch_shapes=[pltpu.VMEM((tm, D_MODEL), BF16),
                        pltpu.VMEM((tm + HALO, POOL_WIDTH), F32),
                        pltpu.VMEM((tm, POOL_WIDTH), BF16)],
        compiler_params=pltpu.CompilerParams(dimension_semantics=("arbitrary",),
                                             vmem_limit_bytes=VMEM_LIMIT_BYTES),
        name="input_stage",
    )(x, g, w, bg, mix, ps, wbp)


def _attention_schedule(lp):
    kinds = {None: 0}
    blocks = []
    for i in range(-(-lp // Q_BLOCK)):
        n = i * Q_BLOCK // K_BLOCK + 1
        for j in range(n):
            if j < n - 1:
                blocks.append((i, j * K_BLOCK, 0))
                continue
            start = min(j * K_BLOCK, lp - K_BLOCK)
            key = (j * K_BLOCK - start, i * Q_BLOCK - start)
            blocks.append((i, start, kinds.setdefault(key, len(kinds))))
    c = np.arange(K_BLOCK)[:, None]
    r = np.arange(Q_BLOCK)[None, :]
    mask = np.zeros((len(kinds), K_BLOCK, Q_BLOCK), np.float32)
    for key, kind in kinds.items():
        if key is not None:
            lo, off = key
            mask[kind] = np.where((c >= lo) & (c < r + off), 0.0, MASKED_LOGIT)
    return np.asarray(blocks, np.int32), mask


def _attention_kernel(n_blocks, sched_ref, q_ref, k_ref, v_ref, tri_ref, mask_ref, eye_ref, o_ref,
                      qs_ref, acc_ref, z_ref, sp_ref, incl_ref, w_ref):
    n_q = qs_ref.shape[0]
    seq_len = q_ref.shape[1]
    lane = lax.broadcasted_iota(jnp.int32, (Q_BLOCK, LANES), 1)
    for i in range(n_q):
        n_rows = min(Q_BLOCK, seq_len - i * Q_BLOCK)
        q2 = q_ref[0, i * Q_BLOCK:i * Q_BLOCK + n_rows, :]
        if n_rows < Q_BLOCK:
            q2 = jnp.concatenate([q2, jnp.zeros((Q_BLOCK - n_rows, LANES), q2.dtype)], axis=0)
        zero = jnp.zeros_like(q2)
        qs_ref[i, 0:Q_BLOCK, :] = jnp.where(lane < SB_HEAD_DIM, q2, zero)
        qs_ref[i, Q_BLOCK:2 * Q_BLOCK, :] = jnp.where(lane >= SB_HEAD_DIM, q2, zero)
    acc_ref[...] = jnp.zeros_like(acc_ref)

    def block(t):
        return (sched_ref[t], pl.multiple_of(sched_ref[n_blocks + t], BF16_ROWS),
                sched_ref[2 * n_blocks + t])

    def raw_scores(t, slot):
        i, start, kind = block(t)
        lhs = jnp.concatenate([qs_ref[i], eye_ref[...]], axis=1)
        rhs = jnp.concatenate([k_ref[0, pl.ds(start, K_BLOCK), :], mask_ref[kind]], axis=1)
        z_ref[slot] = lax.dot_general(lhs, rhs, (((1,), (1,)), ((), ())),
                                      preferred_element_type=F32)

    def softplus(t, slot):
        z = z_ref[slot].astype(BF16)
        sp_ref[slot] = jnp.maximum(z, 0.0) + jnp.log(1.0 + jnp.exp(-jnp.abs(z)))

    def suffix_sums(t, slot):
        incl_ref[slot] = _dot(sp_ref[slot], tri_ref[...])

    def weights(t, slot):
        w_ref[slot] = jnp.exp(z_ref[slot] + incl_ref[slot]).astype(BF16)

    def accumulate(t, slot):
        i, start, _ = block(t)
        decay = jnp.exp(incl_ref[slot, :, 0:1])
        pv = _dot(w_ref[slot], v_ref[0, pl.ds(start, K_BLOCK), :])
        acc_ref[i] = acc_ref[i] * decay + pv

    stages = (raw_scores, softplus, suffix_sums, weights, accumulate)
    depth = len(stages) - 1

    def trip(t, t_mod_slots):
        for s in range(depth, -1, -1):
            if isinstance(t, int) and not 0 <= t - s < n_blocks:
                continue
            stages[s](t - s, (t_mod_slots - s) % ATTN_SLOTS)

    for t in range(depth):
        trip(t, t % ATTN_SLOTS)

    n_groups = (n_blocks - depth) // ATTN_SLOTS

    def body(g, c):
        t0 = depth + g * ATTN_SLOTS
        for u in range(ATTN_SLOTS):
            trip(t0 + u, (depth + u) % ATTN_SLOTS)
        return c

    lax.fori_loop(0, n_groups, body, 0)

    for t in range(depth + n_groups * ATTN_SLOTS, n_blocks + depth):
        trip(t, t % ATTN_SLOTS)

    for i in range(n_q):
        n_rows = min(Q_BLOCK, seq_len - i * Q_BLOCK)
        acc = acc_ref[i]
        both = jnp.where(lane < SB_HEAD_DIM, acc[:Q_BLOCK], acc[Q_BLOCK:]).astype(o_ref.dtype)
        o_ref[0, i * Q_BLOCK:i * Q_BLOCK + n_rows, :] = both[:n_rows]


def _attention(q, k, v):
    b, lp, _ = q.shape
    n_pairs = SB_WIDTH // LANES
    assert lp % BF16_ROWS == 0 and lp >= K_BLOCK, lp
    n_q = -(-lp // Q_BLOCK)
    blocks, mask = _attention_schedule(lp)
    n_blocks = blocks.shape[0]
    sched = jnp.asarray(blocks.T.reshape(-1))
    j = np.arange(K_BLOCK)[:, None]
    s = np.arange(K_BLOCK)[None, :]
    tri = jnp.asarray(np.where(j >= s, -1.0, 0.0), BF16)
    eye = jnp.asarray(np.tile(np.eye(Q_BLOCK), (2, 1)), BF16)
    mask = jnp.asarray(mask, BF16)
    seq = lambda: pl.BlockSpec((1, lp, LANES), lambda bi, hp, sched: (bi, 0, hp))
    return pl.pallas_call(
        functools.partial(_attention_kernel, n_blocks),
        grid_spec=pltpu.PrefetchScalarGridSpec(
            num_scalar_prefetch=1,
            grid=(b, n_pairs),
            in_specs=[seq(), seq(), seq(),
                      pl.BlockSpec(tri.shape, lambda bi, hp, sched: (0, 0)),
                      pl.BlockSpec(mask.shape, lambda bi, hp, sched: (0, 0, 0)),
                      pl.BlockSpec(eye.shape, lambda bi, hp, sched: (0, 0))],
            out_specs=seq(),
            scratch_shapes=[pltpu.VMEM((n_q, 2 * Q_BLOCK, LANES), BF16),
                            pltpu.VMEM((n_q, 2 * Q_BLOCK, LANES), F32),
                            pltpu.VMEM((ATTN_SLOTS, 2 * Q_BLOCK, K_BLOCK), F32),
                            pltpu.VMEM((ATTN_SLOTS, 2 * Q_BLOCK, K_BLOCK), BF16),
                            pltpu.VMEM((ATTN_SLOTS, 2 * Q_BLOCK, K_BLOCK), F32),
                            pltpu.VMEM((ATTN_SLOTS, 2 * Q_BLOCK, K_BLOCK), BF16)]),
        out_shape=jax.ShapeDtypeStruct((b, lp, SB_WIDTH), BF16),
        compiler_params=pltpu.CompilerParams(dimension_semantics=("arbitrary", "arbitrary")),
        name="stick_breaking_attention",
    )(sched, q, k, v, tri, mask, eye)


def _output_stage_kernel(final_norm, x_ref, s_ref, p_ref, gsb_ref, wbs_ref, wo_ref, g2_ref,
                         wfi_ref, wfo_ref, gf_ref, o_ref, m_ref, x1_ref, h2_ref, act_ref):
    s = s_ref[...]
    for c in range(D_MODEL // COL_CHUNK):
        cols = slice(c * COL_CHUNK, (c + 1) * COL_CHUNK)
        m_ref[:, cols] = (p_ref[:, cols] + gsb_ref[:, cols] * _dot(s, wbs_ref[:, cols])).astype(BF16)
    m = m_ref[...]
    for c in range(D_MODEL // COL_CHUNK):
        cols = slice(c * COL_CHUNK, (c + 1) * COL_CHUNK)
        x1_ref[:, cols] = x_ref[:, cols] + _dot(m, wo_ref[:, cols])
    h2_ref[...] = _rms_norm_rows(x1_ref[...], g2_ref[...]).astype(BF16)
    h2 = h2_ref[...]
    for c in range(D_FF // FF_CHUNK):
        cols = slice(c * FF_CHUNK, (c + 1) * FF_CHUNK)
        gt = _dot(h2, wfi_ref[:, cols])
        up = _dot(h2, wfi_ref[:, D_FF + c * FF_CHUNK:D_FF + (c + 1) * FF_CHUNK])
        act_ref[:, cols] = (gt * _sigmoid(gt) * up).astype(BF16)
    act = act_ref[...]
    for c in range(D_MODEL // COL_CHUNK):
        cols = slice(c * COL_CHUNK, (c + 1) * COL_CHUNK)
        y = x1_ref[:, cols] + _dot(act, wfo_ref[:, cols])
        if final_norm:
            x1_ref[:, cols] = y
        else:
            o_ref[:, cols] = y
    if final_norm:
        o_ref[...] = _rms_norm_rows(x1_ref[...], gf_ref[...])


def _output_stage(x, s, p, gsb, wbs, wo, g2, wfi, wfo, gf, *, layer, tm, row0, final_norm):
    b, lp, _ = x.shape
    n_tiles = (lp - row0) // tm
    full = functools.partial(_layer_spec, layer, pipeline_mode=pl.Buffered(1))
    if row0 == 0:
        rows = lambda width: pl.BlockSpec((None, tm, width), lambda bi, j: (bi, j, 0))
    else:
        rows = lambda width: pl.BlockSpec((pl.Squeezed(), pl.Element(tm), pl.Element(width)),
                                          lambda bi, j: (bi, pl.multiple_of(row0 + j * tm, BF16_ROWS), 0))
    return pl.pallas_call(
        functools.partial(_output_stage_kernel, final_norm),
        grid=(b, n_tiles),
        in_specs=[rows(D_MODEL), rows(SB_WIDTH), rows(D_MODEL), rows(D_MODEL),
                  full((SB_WIDTH, D_MODEL)), full((D_MODEL, D_MODEL)), full((1, D_MODEL)),
                  full((D_MODEL, 2 * D_FF)), full((D_FF, D_MODEL)),
                  pl.BlockSpec((1, D_MODEL), lambda bi, j: (0, 0))],
        out_specs=pl.BlockSpec((None, tm, D_MODEL), lambda bi, j: (bi, j, 0)),
        out_shape=jax.ShapeDtypeStruct((b, n_tiles * tm, D_MODEL), F32),
        scratch_shapes=[pltpu.VMEM((tm, D_MODEL), BF16),
                        pltpu.VMEM((tm, D_MODEL), F32),
                        pltpu.VMEM((tm, D_MODEL), BF16),
                        pltpu.VMEM((tm, D_FF), BF16)],
        compiler_params=pltpu.CompilerParams(dimension_semantics=("arbitrary", "arbitrary"),
                                             vmem_limit_bytes=VMEM_LIMIT_BYTES),
        name="output_stage",
    )(x, s, p, gsb, wbs, wo, g2, wfi, wfo, gf)


def kernel(x, meta_tokens, norm1_g, w_in, b_gate, pool_mix, pool_scale, w_branch_pool, w_branch_sb,
           w_out, norm2_g, w_ffn_in, w_ffn_out, final_norm_g):
    b, seq, d = x.shape
    depth = w_in.shape[0]
    lp = N_META + seq
    tiles_per_seq = 3
    tm = lp // tiles_per_seq
    assert tm * tiles_per_seq == lp and tm % BF16_ROWS == 0 and tm >= HALO, (lp, tm)
    assert seq % FINAL_TILE == 0 and N_META % BF16_ROWS == 0, (seq, N_META)

    meta = jnp.broadcast_to(meta_tokens[None].astype(x.dtype), (b, N_META, d))
    hcat = jnp.concatenate([meta, x], axis=1).reshape(b * lp, d)

    w_in, pool_mix, w_branch_pool, w_branch_sb, w_out, w_ffn_in, w_ffn_out = (
        w.astype(BF16) for w in (w_in, pool_mix, w_branch_pool, w_branch_sb, w_out, w_ffn_in, w_ffn_out))
    norm1_g, b_gate, pool_scale, norm2_g = (
        a[:, None, :] for a in (norm1_g, b_gate, pool_scale, norm2_g))

    per_seq = lambda a: a.reshape(b, lp, a.shape[-1])
    for layer in range(depth):
        last = layer == depth - 1
        q, k, v, p, gsb = _input_stage(
            hcat, norm1_g, w_in, b_gate, pool_mix, pool_scale, w_branch_pool,
            layer=layer, tm=tm, tiles_per_seq=tiles_per_seq)
        s = _attention(per_seq(q), per_seq(k), per_seq(v))
        out = _output_stage(
            per_seq(hcat), s, per_seq(p), per_seq(gsb), w_branch_sb, w_out, norm2_g, w_ffn_in,
            w_ffn_out, final_norm_g[None], layer=layer, final_norm=last,
            tm=FINAL_TILE if last else tm, row0=N_META if last else 0)
        hcat = out.reshape(-1, d)

    return out
```

```python
import functools

import numpy as np

import jax
import jax.numpy as jnp
from jax import lax
from jax.experimental import pallas as pl
from jax.experimental.pallas import tpu as pltpu

D_MODEL = 1024
N_META = 16
SB_HEAD_DIM = 64
SB_WIDTH = 512
POOL_WINDOWS = (2, 4, 8, 16)
POOL_WIDTH = 512
POOL_GROUP_DIM = 128
N_IN = 4096
D_FF = 2816
RMS_EPS = 1e-6

LANES = 128
BF16_ROWS = 16
Q_BLOCK = 128
K_BLOCK = 256
ATTN_SLOTS = 8
HALO = 16
FINAL_TILE = 512
FF_CHUNK = 256
COL_CHUNK = 512
VMEM_LIMIT_BYTES = 56 * 1024 * 1024
MASKED_LOGIT = -1e30

BF16 = jnp.bfloat16
F32 = jnp.float32


def _dot(a, b):
    return jnp.dot(a, b, preferred_element_type=F32)


def _rms_norm_rows(x, g):
    ms = jnp.mean(x * x, axis=-1, keepdims=True)
    return x * lax.rsqrt(ms + RMS_EPS) * g


def _sigmoid(x):
    return 1.0 / (1.0 + jnp.exp(-x))


def _input_stage_kernel(tiles_per_seq, x_ref, g_ref, w_ref, bg_ref, mix_ref, ps_ref, wbp_ref,
                        q_ref, k_ref, v_ref, p_ref, gsb_ref,
                        h_ref, ubuf_ref, a_ref):
    tm = x_ref.shape[0]
    tile_in_seq = pl.program_id(0) % tiles_per_seq

    h_ref[...] = _rms_norm_rows(x_ref[...], g_ref[...]).astype(BF16)

    @pl.when(tile_in_seq == 0)
    def _():
        ubuf_ref[0:HALO, :] = jnp.zeros((HALO, POOL_WIDTH), F32)

    @pl.when(tile_in_seq != 0)
    def _():
        ubuf_ref[0:HALO, :] = ubuf_ref[tm:tm + HALO, :]

    h = h_ref[...]
    ubuf_ref[HALO:HALO + tm, :] = _dot(h, w_ref[:, 3 * SB_WIDTH:3 * SB_WIDTH + POOL_WIDTH])

    def gate(branch_ref, c):
        first = D_MODEL * (branch_ref is gsb_ref) + c * COL_CHUNK
        w0 = 3 * SB_WIDTH + POOL_WIDTH + first
        branch_ref[:, c * COL_CHUNK:(c + 1) * COL_CHUNK] = _sigmoid(
            _dot(h, w_ref[:, w0:w0 + COL_CHUNK]) + bg_ref[:, first:first + COL_CHUNK])

    gates = [(ref, c) for c in range(D_MODEL // COL_CHUNK) for ref in (p_ref, gsb_ref)]
    pos = tile_in_seq * tm + lax.broadcasted_iota(jnp.int32, (tm, 1), 0)
    for g, window in enumerate(POOL_WINDOWS):
        gate(*gates[g])
        cols = slice(g * POOL_GROUP_DIM, (g + 1) * POOL_GROUP_DIM)
        ug = ubuf_ref[HALO:HALO + tm, cols]
        total = ug
        for d in range(1, window):
            total = total + ubuf_ref[HALO - d:HALO - d + tm, cols]
        cnt = jnp.minimum(pos + 1, window).astype(F32)
        diff = (total / cnt - ug).astype(BF16)
        a_ref[:, cols] = (_dot(diff, mix_ref[g]) * ps_ref[:, cols]).astype(BF16)
    for extra in gates[len(POOL_WINDOWS):]:
        gate(*extra)

    q_ref[...] = (_dot(h, w_ref[:, 0:SB_WIDTH]) * (SB_HEAD_DIM ** -0.5)).astype(BF16)
    k_ref[...] = _dot(h, w_ref[:, SB_WIDTH:2 * SB_WIDTH]).astype(BF16)
    v_ref[...] = _dot(h, w_ref[:, 2 * SB_WIDTH:3 * SB_WIDTH]).astype(BF16)

    a = a_ref[...]
    for c in range(D_MODEL // COL_CHUNK):
        cols = slice(c * COL_CHUNK, (c + 1) * COL_CHUNK)
        p_ref[:, cols] = p_ref[:, cols] * _dot(a, wbp_ref[:, cols])


def _layer_spec(layer, shape, **kwargs):
    return pl.BlockSpec((None,) + shape, lambda *_: (layer,) + (0,) * len(shape), **kwargs)


def _input_stage(x, g, w, bg, mix, ps, wbp, *, layer, tm, tiles_per_seq):
    t = x.shape[0]
    full = functools.partial(_layer_spec, layer)
    rows = lambda width: pl.BlockSpec((tm, width), lambda i: (i, 0))
    return pl.pallas_call(
        functools.partial(_input_stage_kernel, tiles_per_seq),
        grid=(t // tm,),
        in_specs=[rows(D_MODEL), full((1, D_MODEL)), full((D_MODEL, N_IN)), full((1, 2 * D_MODEL)),
                  full((len(POOL_WINDOWS), POOL_GROUP_DIM, POOL_GROUP_DIM)), full((1, POOL_WIDTH)),
                  full((POOL_WIDTH, D_MODEL))],
        out_specs=[rows(SB_WIDTH), rows(SB_WIDTH), rows(SB_WIDTH), rows(D_MODEL), rows(D_MODEL)],
        out_shape=[jax.ShapeDtypeStruct((t, SB_WIDTH), BF16)] * 3
                  + [jax.ShapeDtypeStruct((t, D_MODEL), F32)] * 2,
        scratch_shapes=[pltpu.VMEM((tm, D_MODEL), BF16),
                        pltpu.VMEM((tm + HALO, POOL_WIDTH), F32),
                        pltpu.VMEM((tm, POOL_WIDTH), BF16)],
        compiler_params=pltpu.CompilerParams(dimension_semantics=("arbitrary",),
                                             vmem_limit_bytes=VMEM_LIMIT_BYTES),
        name="input_stage",
    )(x, g, w, bg, mix, ps, wbp)


def _attention_schedule(lp):
    kinds = {None: 0}
    blocks = []
    for i in range(lp // Q_BLOCK):
        n = i * Q_BLOCK // K_BLOCK + 1
        for j in range(n):
            if j < n - 1:
                blocks.append((i, j * K_BLOCK // Q_BLOCK, 0))
                continue
            start = min(j * K_BLOCK, lp - K_BLOCK)
            key = (j * K_BLOCK - start, i * Q_BLOCK - start)
            blocks.append((i, start // Q_BLOCK, kinds.setdefault(key, len(kinds))))
    c = np.arange(K_BLOCK)[None, :]
    r = np.arange(Q_BLOCK)[:, None]
    mask = np.zeros((len(kinds), Q_BLOCK, K_BLOCK), np.float32)
    for key, kind in kinds.items():
        if key is not None:
            lo, off = key
            mask[kind] = np.where((c >= lo) & (c < r + off), 0.0, MASKED_LOGIT)
    return np.asarray(blocks, np.int32), mask


def _attention_kernel(n_blocks, sched_ref, q_ref, k_ref, v_ref, tri_ref, mask_ref, eye_ref, o_ref,
                      qs_ref, kt_ref, vc_ref, acc_ref, z_ref, sp_ref, incl_ref, w_ref):
    n_q = qs_ref.shape[0]
    seq_len = q_ref.shape[1]
    lane = lax.broadcasted_iota(jnp.int32, (Q_BLOCK, LANES), 1)

    def chunk(ref, i):
        n_rows = min(Q_BLOCK, seq_len - i * Q_BLOCK)
        rows = ref[0, i * Q_BLOCK:i * Q_BLOCK + n_rows, :]
        if n_rows < Q_BLOCK:
            rows = jnp.concatenate([rows, jnp.zeros((Q_BLOCK - n_rows, LANES), rows.dtype)], axis=0)
        return rows

    for i in range(n_q):
        q2 = chunk(q_ref, i)
        zero = jnp.zeros_like(q2)
        qs_ref[i, 0:Q_BLOCK, :] = jnp.where(lane < SB_HEAD_DIM, q2, zero)
        qs_ref[i, Q_BLOCK:2 * Q_BLOCK, :] = jnp.where(lane >= SB_HEAD_DIM, q2, zero)
        kt_ref[i] = chunk(k_ref, i).T
        vc_ref[i] = chunk(v_ref, i)
    acc_ref[...] = jnp.zeros_like(acc_ref)

    def block(t):
        return sched_ref[t], sched_ref[n_blocks + t], sched_ref[2 * n_blocks + t]

    def raw_scores(t, slot):
        i, c, kind = block(t)
        lhs = jnp.concatenate([qs_ref[i], eye_ref[...]], axis=1)
        keys_t = jnp.concatenate([kt_ref[c], kt_ref[c + 1]], axis=1)
        z_ref[slot] = _dot(lhs, jnp.concatenate([keys_t, mask_ref[kind]], axis=0))

    def softplus(t, slot):
        z = z_ref[slot].astype(BF16)
        sp_ref[slot] = jnp.maximum(z, 0.0) + jnp.log(1.0 + jnp.exp(-jnp.abs(z)))

    def suffix_sums(t, slot):
        incl_ref[slot] = _dot(sp_ref[slot], tri_ref[...])

    def weights(t, slot):
        w_ref[slot] = jnp.exp(z_ref[slot] + incl_ref[slot]).astype(BF16)

    def accumulate(t, slot):
        i, c, _ = block(t)
        decay = jnp.exp(incl_ref[slot, :, 0:1])
        values = jnp.concatenate([vc_ref[c], vc_ref[c + 1]], axis=0)
        acc_ref[i] = acc_ref[i] * decay + _dot(w_ref[slot], values)

    stages = (raw_scores, softplus, suffix_sums, weights, accumulate)
    depth = len(stages) - 1

    def trip(t, t_mod_slots):
        for s in range(depth, -1, -1):
            if isinstance(t, int) and not 0 <= t - s < n_blocks:
                continue
            stages[s](t - s, (t_mod_slots - s) % ATTN_SLOTS)

    for t in range(depth):
        trip(t, t % ATTN_SLOTS)

    n_groups = (n_blocks - depth) // ATTN_SLOTS

    def body(g, c):
        t0 = depth + g * ATTN_SLOTS
        for u in range(ATTN_SLOTS):
            trip(t0 + u, (depth + u) % ATTN_SLOTS)
        return c

    lax.fori_loop(0, n_groups, body, 0)

    for t in range(depth + n_groups * ATTN_SLOTS, n_blocks + depth):
        trip(t, t % ATTN_SLOTS)

    for i in range(n_q):
        n_rows = min(Q_BLOCK, seq_len - i * Q_BLOCK)
        acc = acc_ref[i]
        both = jnp.where(lane < SB_HEAD_DIM, acc[:Q_BLOCK], acc[Q_BLOCK:]).astype(o_ref.dtype)
        o_ref[0, i * Q_BLOCK:i * Q_BLOCK + n_rows, :] = both[:n_rows]


def _attention(q, k, v):
    b, lp, _ = q.shape
    n_pairs = SB_WIDTH // LANES
    assert lp % BF16_ROWS == 0 and lp >= K_BLOCK, lp
    n_q = -(-lp // Q_BLOCK)
    blocks, mask = _attention_schedule(n_q * Q_BLOCK)
    n_blocks = blocks.shape[0]
    sched = jnp.asarray(blocks.T.reshape(-1))
    j = np.arange(K_BLOCK)[:, None]
    s = np.arange(K_BLOCK)[None, :]
    tri = jnp.asarray(np.where(j >= s, -1.0, 0.0), BF16)
    eye = jnp.asarray(np.tile(np.eye(Q_BLOCK), (2, 1)), BF16)
    mask = jnp.asarray(mask, BF16)
    seq = lambda: pl.BlockSpec((1, lp, LANES), lambda bi, hp, sched: (bi, 0, hp))
    return pl.pallas_call(
        functools.partial(_attention_kernel, n_blocks),
        grid_spec=pltpu.PrefetchScalarGridSpec(
            num_scalar_prefetch=1,
            grid=(b, n_pairs),
            in_specs=[seq(), seq(), seq(),
                      pl.BlockSpec(tri.shape, lambda bi, hp, sched: (0, 0)),
                      pl.BlockSpec(mask.shape, lambda bi, hp, sched: (0, 0, 0)),
                      pl.BlockSpec(eye.shape, lambda bi, hp, sched: (0, 0))],
            out_specs=seq(),
            scratch_shapes=[pltpu.VMEM((n_q, 2 * Q_BLOCK, LANES), BF16),
                            pltpu.VMEM((n_q, LANES, Q_BLOCK), BF16),
                            pltpu.VMEM((n_q, Q_BLOCK, LANES), BF16),
                            pltpu.VMEM((n_q, 2 * Q_BLOCK, LANES), F32),
                            pltpu.VMEM((ATTN_SLOTS, 2 * Q_BLOCK, K_BLOCK), F32),
                            pltpu.VMEM((ATTN_SLOTS, 2 * Q_BLOCK, K_BLOCK), BF16),
                            pltpu.VMEM((ATTN_SLOTS, 2 * Q_BLOCK, K_BLOCK), F32),
                            pltpu.VMEM((ATTN_SLOTS, 2 * Q_BLOCK, K_BLOCK), BF16)]),
        out_shape=jax.ShapeDtypeStruct((b, lp, SB_WIDTH), BF16),
        compiler_params=pltpu.CompilerParams(dimension_semantics=("arbitrary", "arbitrary")),
        name="stick_breaking_attention",
    )(sched, q, k, v, tri, mask, eye)


def _output_stage_kernel(final_norm, x_ref, s_ref, p_ref, gsb_ref, wbs_ref, wo_ref, g2_ref,
                         wfi_ref, wfo_ref, gf_ref, o_ref, m_ref, x1_ref, h2_ref, act_ref):
    s = s_ref[...]
    for c in range(D_MODEL // COL_CHUNK):
        cols = slice(c * COL_CHUNK, (c + 1) * COL_CHUNK)
        m_ref[:, cols] = (p_ref[:, cols] + gsb_ref[:, cols] * _dot(s, wbs_ref[:, cols])).astype(BF16)
    m = m_ref[...]
    for c in range(D_MODEL // COL_CHUNK):
        cols = slice(c * COL_CHUNK, (c + 1) * COL_CHUNK)
        x1_ref[:, cols] = x_ref[:, cols] + _dot(m, wo_ref[:, cols])
    h2_ref[...] = _rms_norm_rows(x1_ref[...], g2_ref[...]).astype(BF16)
    h2 = h2_ref[...]
    for c in range(D_FF // FF_CHUNK):
        cols = slice(c * FF_CHUNK, (c + 1) * FF_CHUNK)
        gt = _dot(h2, wfi_ref[:, cols])
        up = _dot(h2, wfi_ref[:, D_FF + c * FF_CHUNK:D_FF + (c + 1) * FF_CHUNK])
        act_ref[:, cols] = (gt * _sigmoid(gt) * up).astype(BF16)
    act = act_ref[...]
    for c in range(D_MODEL // COL_CHUNK):
        cols = slice(c * COL_CHUNK, (c + 1) * COL_CHUNK)
        y = x1_ref[:, cols] + _dot(act, wfo_ref[:, cols])
        if final_norm:
            x1_ref[:, cols] = y
        else:
            o_ref[:, cols] = y
    if final_norm:
        o_ref[...] = _rms_norm_rows(x1_ref[...], gf_ref[...])


def _output_stage(x, s, p, gsb, wbs, wo, g2, wfi, wfo, gf, *, layer, tm, row0, final_norm):
    b, lp, _ = x.shape
    n_tiles = (lp - row0) // tm
    full = functools.partial(_layer_spec, layer, pipeline_mode=pl.Buffered(1))
    if row0 == 0:
        rows = lambda width: pl.BlockSpec((None, tm, width), lambda bi, j: (bi, j, 0))
    else:
        rows = lambda width: pl.BlockSpec((pl.Squeezed(), pl.Element(tm), pl.Element(width)),
                                          lambda bi, j: (bi, pl.multiple_of(row0 + j * tm, BF16_ROWS), 0))
    return pl.pallas_call(
        functools.partial(_output_stage_kernel, final_norm),
        grid=(b, n_tiles),
        in_specs=[rows(D_MODEL), rows(SB_WIDTH), rows(D_MODEL), rows(D_MODEL),
                  full((SB_WIDTH, D_MODEL)), full((D_MODEL, D_MODEL)), full((1, D_MODEL)),
                  full((D_MODEL, 2 * D_FF)), full((D_FF, D_MODEL)),
                  pl.BlockSpec((1, D_MODEL), lambda bi, j: (0, 0))],
        out_specs=pl.BlockSpec((None, tm, D_MODEL), lambda bi, j: (bi, j, 0)),
        out_shape=jax.ShapeDtypeStruct((b, n_tiles * tm, D_MODEL), F32),
        scratch_shapes=[pltpu.VMEM((tm, D_MODEL), BF16),
                        pltpu.VMEM((tm, D_MODEL), F32),
                        pltpu.VMEM((tm, D_MODEL), BF16),
                        pltpu.VMEM((tm, D_FF), BF16)],
        compiler_params=pltpu.CompilerParams(dimension_semantics=("arbitrary", "arbitrary"),
                                             vmem_limit_bytes=VMEM_LIMIT_BYTES),
        name="output_stage",
    )(x, s, p, gsb, wbs, wo, g2, wfi, wfo, gf)


def kernel(x, meta_tokens, norm1_g, w_in, b_gate, pool_mix, pool_scale, w_branch_pool, w_branch_sb,
           w_out, norm2_g, w_ffn_in, w_ffn_out, final_norm_g):
    b, seq, d = x.shape
    depth = w_in.shape[0]
    lp = N_META + seq
    tiles_per_seq = 3
    tm = lp // tiles_per_seq
    assert tm * tiles_per_seq == lp and tm % BF16_ROWS == 0 and tm >= HALO, (lp, tm)
    assert seq % FINAL_TILE == 0 and N_META % BF16_ROWS == 0, (seq, N_META)

    meta = jnp.broadcast_to(meta_tokens[None].astype(x.dtype), (b, N_META, d))
    hcat = jnp.concatenate([meta, x], axis=1).reshape(b * lp, d)

    w_in, pool_mix, w_branch_pool, w_branch_sb, w_out, w_ffn_in, w_ffn_out = (
        w.astype(BF16) for w in (w_in, pool_mix, w_branch_pool, w_branch_sb, w_out, w_ffn_in, w_ffn_out))
    norm1_g, b_gate, pool_scale, norm2_g = (
        a[:, None, :] for a in (norm1_g, b_gate, pool_scale, norm2_g))

    per_seq = lambda a: a.reshape(b, lp, a.shape[-1])
    for layer in range(depth):
        last = layer == depth - 1
        q, k, v, p, gsb = _input_stage(
            hcat, norm1_g, w_in, b_gate, pool_mix, pool_scale, w_branch_pool,
            layer=layer, tm=tm, tiles_per_seq=tiles_per_seq)
        s = _attention(per_seq(q), per_seq(k), per_seq(v))
        out = _output_stage(
            per_seq(hcat), s, per_seq(p), per_seq(gsb), w_branch_sb, w_out, norm2_g, w_ffn_in,
            w_ffn_out, final_norm_g[None], layer=layer, final_norm=last,
            tm=FINAL_TILE if last else tm, row0=N_META if last else 0)
        hcat = out.reshape(-1, d)

    return out
```

```python
import functools

import numpy as np

import jax
import jax.numpy as jnp
from jax import lax
from jax.experimental import pallas as pl
from jax.experimental.pallas import tpu as pltpu

D_MODEL = 1024
N_META = 16
SB_HEAD_DIM = 64
SB_WIDTH = 512
POOL_WINDOWS = (2, 4, 8, 16)
POOL_WIDTH = 512
POOL_GROUP_DIM = 128
N_IN = 4096
D_FF = 2816
RMS_EPS = 1e-6

LANES = 128
BF16_ROWS = 16
Q_BLOCK = 128
K_BLOCK = 256
ATTN_SLOTS = 8
HALO = 16
FINAL_TILE = 512
FF_CHUNK = 256
COL_CHUNK = 512
VMEM_LIMIT_BYTES = 56 * 1024 * 1024
MASKED_LOGIT = -1e30

BF16 = jnp.bfloat16
F32 = jnp.float32


def _dot(a, b):
    return jnp.dot(a, b, preferred_element_type=F32)


def _rms_norm_rows(x, g):
    ms = jnp.mean(x * x, axis=-1, keepdims=True)
    return x * lax.rsqrt(ms + RMS_EPS) * g


def _sigmoid(x):
    return 1.0 / (1.0 + jnp.exp(-x))


def _input_stage_kernel(tiles_per_seq, frames_only, x_ref, meta_ref, g_ref, w_ref, bg_ref, mix_ref,
                        ps_ref, wbp_ref, q_ref, k_ref, v_ref, p_ref, gsb_ref,
                        h_ref, ubuf_ref, a_ref):
    tm = x_ref.shape[0]
    tile_in_seq = pl.program_id(0) % tiles_per_seq
    n_meta = meta_ref.shape[0]

    @pl.when(tile_in_seq == 0)
    def _():
        ubuf_ref[0:HALO, :] = jnp.zeros((HALO, POOL_WIDTH), F32)
        if frames_only:
            h_ref[0:n_meta, :] = _rms_norm_rows(meta_ref[...], g_ref[...]).astype(BF16)
            h_ref[n_meta:tm, :] = _rms_norm_rows(x_ref[0:tm - n_meta, :], g_ref[...]).astype(BF16)

    @pl.when(tile_in_seq != 0)
    def _():
        ubuf_ref[0:HALO, :] = ubuf_ref[tm:tm + HALO, :]
        if frames_only:
            h_ref[...] = _rms_norm_rows(x_ref[...], g_ref[...]).astype(BF16)

    if not frames_only:
        h_ref[...] = _rms_norm_rows(x_ref[...], g_ref[...]).astype(BF16)

    h = h_ref[...]
    ubuf_ref[HALO:HALO + tm, :] = _dot(h, w_ref[:, 3 * SB_WIDTH:3 * SB_WIDTH + POOL_WIDTH])

    def gate(branch_ref, c):
        first = D_MODEL * (branch_ref is gsb_ref) + c * COL_CHUNK
        w0 = 3 * SB_WIDTH + POOL_WIDTH + first
        branch_ref[:, c * COL_CHUNK:(c + 1) * COL_CHUNK] = _sigmoid(
            _dot(h, w_ref[:, w0:w0 + COL_CHUNK]) + bg_ref[:, first:first + COL_CHUNK])

    gates = [(ref, c) for c in range(D_MODEL // COL_CHUNK) for ref in (p_ref, gsb_ref)]
    pos = tile_in_seq * tm + lax.broadcasted_iota(jnp.int32, (tm, 1), 0)
    for g, window in enumerate(POOL_WINDOWS):
        gate(*gates[g])
        cols = slice(g * POOL_GROUP_DIM, (g + 1) * POOL_GROUP_DIM)
        ug = ubuf_ref[HALO:HALO + tm, cols]
        total = ug
        for d in range(1, window):
            total = total + ubuf_ref[HALO - d:HALO - d + tm, cols]
        cnt = jnp.minimum(pos + 1, window).astype(F32)
        diff = (total / cnt - ug).astype(BF16)
        a_ref[:, cols] = (_dot(diff, mix_ref[g]) * ps_ref[:, cols]).astype(BF16)
    for extra in gates[len(POOL_WINDOWS):]:
        gate(*extra)

    q_ref[...] = (_dot(h, w_ref[:, 0:SB_WIDTH]) * (SB_HEAD_DIM ** -0.5)).astype(BF16)
    k_ref[...] = _dot(h, w_ref[:, SB_WIDTH:2 * SB_WIDTH]).astype(BF16)
    v_ref[...] = _dot(h, w_ref[:, 2 * SB_WIDTH:3 * SB_WIDTH]).astype(BF16)

    a = a_ref[...]
    for c in range(D_MODEL // COL_CHUNK):
        cols = slice(c * COL_CHUNK, (c + 1) * COL_CHUNK)
        p_ref[:, cols] = p_ref[:, cols] * _dot(a, wbp_ref[:, cols])


def _layer_spec(layer, shape, **kwargs):
    return pl.BlockSpec((None,) + shape, lambda *_: (layer,) + (0,) * len(shape), **kwargs)


def _frames_window(tm, n_meta, batch_and_tile):
    def index_map(*grid_idx):
        bi, j = batch_and_tile(*grid_idx)
        return bi, pl.multiple_of(jnp.maximum(j * tm - n_meta, 0), BF16_ROWS), 0
    return pl.BlockSpec((pl.Squeezed(), pl.Element(tm), pl.Element(D_MODEL)), index_map)


def _input_stage(x, meta, g, w, bg, mix, ps, wbp, *, layer, tm, tiles_per_seq):
    frames_only = x.ndim == 3
    t = x.shape[0] * tiles_per_seq * tm if frames_only else x.shape[0]
    full = functools.partial(_layer_spec, layer)
    rows = lambda width: pl.BlockSpec((tm, width), lambda i: (i, 0))
    x_spec = (_frames_window(tm, meta.shape[0], lambda i: (i // tiles_per_seq, i % tiles_per_seq))
              if frames_only else rows(D_MODEL))
    return pl.pallas_call(
        functools.partial(_input_stage_kernel, tiles_per_seq, frames_only),
        grid=(t // tm,),
        in_specs=[x_spec, pl.BlockSpec(meta.shape, lambda i: (0, 0)),
                  full((1, D_MODEL)), full((D_MODEL, N_IN)), full((1, 2 * D_MODEL)),
                  full((len(POOL_WINDOWS), POOL_GROUP_DIM, POOL_GROUP_DIM)), full((1, POOL_WIDTH)),
                  full((POOL_WIDTH, D_MODEL))],
        out_specs=[rows(SB_WIDTH), rows(SB_WIDTH), rows(SB_WIDTH), rows(D_MODEL), rows(D_MODEL)],
        out_shape=[jax.ShapeDtypeStruct((t, SB_WIDTH), BF16)] * 3
                  + [jax.ShapeDtypeStruct((t, D_MODEL), F32)] * 2,
        scratch_shapes=[pltpu.VMEM((tm, D_MODEL), BF16),
                        pltpu.VMEM((tm + HALO, POOL_WIDTH), F32),
                        pltpu.VMEM((tm, POOL_WIDTH), BF16)],
        compiler_params=pltpu.CompilerParams(dimension_semantics=("arbitrary",),
                                             vmem_limit_bytes=VMEM_LIMIT_BYTES),
        name="input_stage",
    )(x, meta, g, w, bg, mix, ps, wbp)


def _attention_schedule(lp):
    kinds = {None: 0}
    blocks = []
    for i in range(lp // Q_BLOCK):
        n = i * Q_BLOCK // K_BLOCK + 1
        for j in range(n):
            if j < n - 1:
                blocks.append((i, j * K_BLOCK // Q_BLOCK, 0))
                continue
            start = min(j * K_BLOCK, lp - K_BLOCK)
            key = (j * K_BLOCK - start, i * Q_BLOCK - start)
            blocks.append((i, start // Q_BLOCK, kinds.setdefault(key, len(kinds))))
    c = np.arange(K_BLOCK)[None, :]
    r = np.arange(Q_BLOCK)[:, None]
    mask = np.zeros((len(kinds), Q_BLOCK, K_BLOCK), np.float32)
    for key, kind in kinds.items():
        if key is not None:
            lo, off = key
            mask[kind] = np.where((c >= lo) & (c < r + off), 0.0, MASKED_LOGIT)
    return np.asarray(blocks, np.int32), mask


def _attention_kernel(n_blocks, sched_ref, q_ref, k_ref, v_ref, tri_ref, mask_ref, eye_ref, o_ref,
                      qs_ref, kt_ref, vc_ref, acc_ref, z_ref, sp_ref, incl_ref, w_ref):
    n_q = qs_ref.shape[0]
    seq_len = q_ref.shape[1]
    lane = lax.broadcasted_iota(jnp.int32, (Q_BLOCK, LANES), 1)

    def chunk(ref, i):
        n_rows = min(Q_BLOCK, seq_len - i * Q_BLOCK)
        rows = ref[0, i * Q_BLOCK:i * Q_BLOCK + n_rows, :]
        if n_rows < Q_BLOCK:
            rows = jnp.concatenate([rows, jnp.zeros((Q_BLOCK - n_rows, LANES), rows.dtype)], axis=0)
        return rows

    for i in range(n_q):
        q2 = chunk(q_ref, i)
        zero = jnp.zeros_like(q2)
        qs_ref[i, 0:Q_BLOCK, :] = jnp.where(lane < SB_HEAD_DIM, q2, zero)
        qs_ref[i, Q_BLOCK:2 * Q_BLOCK, :] = jnp.where(lane >= SB_HEAD_DIM, q2, zero)
        kt_ref[i] = chunk(k_ref, i).T
        vc_ref[i] = chunk(v_ref, i)
    acc_ref[...] = jnp.zeros_like(acc_ref)

    def block(t):
        return sched_ref[t], sched_ref[n_blocks + t], sched_ref[2 * n_blocks + t]

    def raw_scores(t, slot):
        i, c, kind = block(t)
        lhs = jnp.concatenate([qs_ref[i], eye_ref[...]], axis=1)
        keys_t = jnp.concatenate([kt_ref[c], kt_ref[c + 1]], axis=1)
        z_ref[slot] = _dot(lhs, jnp.concatenate([keys_t, mask_ref[kind]], axis=0))

    def softplus(t, slot):
        z = z_ref[slot].astype(BF16)
        sp_ref[slot] = jnp.maximum(z, 0.0) + jnp.log(1.0 + jnp.exp(-jnp.abs(z)))

    def suffix_sums(t, slot):
        incl_ref[slot] = _dot(sp_ref[slot], tri_ref[...])

    def weights(t, slot):
        w_ref[slot] = jnp.exp(z_ref[slot] + incl_ref[slot]).astype(BF16)

    def accumulate(t, slot):
        i, c, _ = block(t)
        decay = jnp.exp(incl_ref[slot, :, 0:1])
        values = jnp.concatenate([vc_ref[c], vc_ref[c + 1]], axis=0)
        acc_ref[i] = acc_ref[i] * decay + _dot(w_ref[slot], values)

    stages = (raw_scores, softplus, suffix_sums, weights, accumulate)
    depth = len(stages) - 1

    def trip(t, t_mod_slots):
        for s in range(depth, -1, -1):
            if isinstance(t, int) and not 0 <= t - s < n_blocks:
                continue
            stages[s](t - s, (t_mod_slots - s) % ATTN_SLOTS)

    for t in range(depth):
        trip(t, t % ATTN_SLOTS)

    n_groups = (n_blocks - depth) // ATTN_SLOTS

    def body(g, c):
        t0 = depth + g * ATTN_SLOTS
        for u in range(ATTN_SLOTS):
            trip(t0 + u, (depth + u) % ATTN_SLOTS)
        return c

    lax.fori_loop(0, n_groups, body, 0)

    for t in range(depth + n_groups * ATTN_SLOTS, n_blocks + depth):
        trip(t, t % ATTN_SLOTS)

    for i in range(n_q):
        n_rows = min(Q_BLOCK, seq_len - i * Q_BLOCK)
        acc = acc_ref[i]
        both = jnp.where(lane < SB_HEAD_DIM, acc[:Q_BLOCK], acc[Q_BLOCK:]).astype(o_ref.dtype)
        o_ref[0, i * Q_BLOCK:i * Q_BLOCK + n_rows, :] = both[:n_rows]


def _attention(q, k, v):
    b, lp, _ = q.shape
    n_pairs = SB_WIDTH // LANES
    assert lp % BF16_ROWS == 0 and lp >= K_BLOCK, lp
    n_q = -(-lp // Q_BLOCK)
    blocks, mask = _attention_schedule(n_q * Q_BLOCK)
    n_blocks = blocks.shape[0]
    sched = jnp.asarray(blocks.T.reshape(-1))
    j = np.arange(K_BLOCK)[:, None]
    s = np.arange(K_BLOCK)[None, :]
    tri = jnp.asarray(np.where(j >= s, -1.0, 0.0), BF16)
    eye = jnp.asarray(np.tile(np.eye(Q_BLOCK), (2, 1)), BF16)
    mask = jnp.asarray(mask, BF16)
    seq = lambda: pl.BlockSpec((1, lp, LANES), lambda bi, hp, sched: (bi, 0, hp))
    return pl.pallas_call(
        functools.partial(_attention_kernel, n_blocks),
        grid_spec=pltpu.PrefetchScalarGridSpec(
            num_scalar_prefetch=1,
            grid=(b, n_pairs),
            in_specs=[seq(), seq(), seq(),
                      pl.BlockSpec(tri.shape, lambda bi, hp, sched: (0, 0)),
                      pl.BlockSpec(mask.shape, lambda bi, hp, sched: (0, 0, 0)),
                      pl.BlockSpec(eye.shape, lambda bi, hp, sched: (0, 0))],
            out_specs=seq(),
            scratch_shapes=[pltpu.VMEM((n_q, 2 * Q_BLOCK, LANES), BF16),
                            pltpu.VMEM((n_q, LANES, Q_BLOCK), BF16),
                            pltpu.VMEM((n_q, Q_BLOCK, LANES), BF16),
                            pltpu.VMEM((n_q, 2 * Q_BLOCK, LANES), F32),
                            pltpu.VMEM((ATTN_SLOTS, 2 * Q_BLOCK, K_BLOCK), F32),
                            pltpu.VMEM((ATTN_SLOTS, 2 * Q_BLOCK, K_BLOCK), BF16),
                            pltpu.VMEM((ATTN_SLOTS, 2 * Q_BLOCK, K_BLOCK), F32),
                            pltpu.VMEM((ATTN_SLOTS, 2 * Q_BLOCK, K_BLOCK), BF16)]),
        out_shape=jax.ShapeDtypeStruct((b, lp, SB_WIDTH), BF16),
        compiler_params=pltpu.CompilerParams(dimension_semantics=("arbitrary", "arbitrary")),
        name="stick_breaking_attention",
    )(sched, q, k, v, tri, mask, eye)


def _output_stage_kernel(final_norm, frames_only, x_ref, meta_ref, s_ref, p_ref, gsb_ref, wbs_ref,
                         wo_ref, g2_ref, wfi_ref, wfo_ref, gf_ref, o_ref,
                         m_ref, x1_ref, h2_ref, act_ref):
    tm = x1_ref.shape[0]
    n_meta = meta_ref.shape[0]
    s = s_ref[...]
    for c in range(D_MODEL // COL_CHUNK):
        cols = slice(c * COL_CHUNK, (c + 1) * COL_CHUNK)
        m_ref[:, cols] = (p_ref[:, cols] + gsb_ref[:, cols] * _dot(s, wbs_ref[:, cols])).astype(BF16)
    m = m_ref[...]
    for c in range(D_MODEL // COL_CHUNK):
        cols = slice(c * COL_CHUNK, (c + 1) * COL_CHUNK)
        merged = _dot(m, wo_ref[:, cols])
        if frames_only:
            @pl.when(pl.program_id(1) == 0)
            def _():
                x1_ref[0:n_meta, cols] = meta_ref[:, cols] + merged[0:n_meta]
                x1_ref[n_meta:tm, cols] = x_ref[0:tm - n_meta, cols] + merged[n_meta:tm]

            @pl.when(pl.program_id(1) != 0)
            def _():
                x1_ref[:, cols] = x_ref[:, cols] + merged
        else:
            x1_ref[:, cols] = x_ref[:, cols] + merged
    h2_ref[...] = _rms_norm_rows(x1_ref[...], g2_ref[...]).astype(BF16)
    h2 = h2_ref[...]
    for c in range(D_FF // FF_CHUNK):
        cols = slice(c * FF_CHUNK, (c + 1) * FF_CHUNK)
        gt = _dot(h2, wfi_ref[:, cols])
        up = _dot(h2, wfi_ref[:, D_FF + c * FF_CHUNK:D_FF + (c + 1) * FF_CHUNK])
        act_ref[:, cols] = (gt * _sigmoid(gt) * up).astype(BF16)
    act = act_ref[...]
    for c in range(D_MODEL // COL_CHUNK):
        cols = slice(c * COL_CHUNK, (c + 1) * COL_CHUNK)
        y = x1_ref[:, cols] + _dot(act, wfo_ref[:, cols])
        if final_norm:
            x1_ref[:, cols] = y
        else:
            o_ref[:, cols] = y
    if final_norm:
        o_ref[...] = _rms_norm_rows(x1_ref[...], gf_ref[...])


def _output_stage(x, meta, s, p, gsb, wbs, wo, g2, wfi, wfo, gf, *, layer, tm, row0, final_norm):
    b, lp, _ = s.shape
    frames_only = x.shape[1] != lp
    assert not (frames_only and row0), "frames-only residual input needs whole-sequence tiles"
    n_tiles = (lp - row0) // tm
    full = functools.partial(_layer_spec, layer, pipeline_mode=pl.Buffered(1))
    if row0 == 0:
        rows = lambda width: pl.BlockSpec((None, tm, width), lambda bi, j: (bi, j, 0))
    else:
        rows = lambda width: pl.BlockSpec((pl.Squeezed(), pl.Element(tm), pl.Element(width)),
                                          lambda bi, j: (bi, pl.multiple_of(row0 + j * tm, BF16_ROWS), 0))
    x_spec = _frames_window(tm, meta.shape[0], lambda bi, j: (bi, j)) if frames_only else rows(D_MODEL)
    return pl.pallas_call(
        functools.partial(_output_stage_kernel, final_norm, frames_only),
        grid=(b, n_tiles),
        in_specs=[x_spec, pl.BlockSpec(meta.shape, lambda bi, j: (0, 0)),
                  rows(SB_WIDTH), rows(D_MODEL), rows(D_MODEL),
                  full((SB_WIDTH, D_MODEL)), full((D_MODEL, D_MODEL)), full((1, D_MODEL)),
                  full((D_MODEL, 2 * D_FF)), full((D_FF, D_MODEL)),
                  pl.BlockSpec((1, D_MODEL), lambda bi, j: (0, 0))],
        out_specs=pl.BlockSpec((None, tm, D_MODEL), lambda bi, j: (bi, j, 0)),
        out_shape=jax.ShapeDtypeStruct((b, n_tiles * tm, D_MODEL), F32),
        scratch_shapes=[pltpu.VMEM((tm, D_MODEL), BF16),
                        pltpu.VMEM((tm, D_MODEL), F32),
                        pltpu.VMEM((tm, D_MODEL), BF16),
                        pltpu.VMEM((tm, D_FF), BF16)],
        compiler_params=pltpu.CompilerParams(dimension_semantics=("arbitrary", "arbitrary"),
                                             vmem_limit_bytes=VMEM_LIMIT_BYTES),
        name="output_stage",
    )(x, meta, s, p, gsb, wbs, wo, g2, wfi, wfo, gf)


def kernel(x, meta_tokens, norm1_g, w_in, b_gate, pool_mix, pool_scale, w_branch_pool, w_branch_sb,
           w_out, norm2_g, w_ffn_in, w_ffn_out, final_norm_g):
    b, seq, d = x.shape
    depth = w_in.shape[0]
    lp = N_META + seq
    tiles_per_seq = 3
    tm = lp // tiles_per_seq
    assert tm * tiles_per_seq == lp and tm % BF16_ROWS == 0 and tm >= HALO, (lp, tm)
    assert seq % FINAL_TILE == 0 and N_META % BF16_ROWS == 0, (seq, N_META)

    assert depth >= 2 and meta_tokens.shape == (N_META, d), (depth, meta_tokens.shape)
    meta = meta_tokens.astype(x.dtype)

    w_in, pool_mix, w_branch_pool, w_branch_sb, w_out, w_ffn_in, w_ffn_out = (
        w.astype(BF16) for w in (w_in, pool_mix, w_branch_pool, w_branch_sb, w_out, w_ffn_in, w_ffn_out))
    norm1_g, b_gate, pool_scale, norm2_g = (
        a[:, None, :] for a in (norm1_g, b_gate, pool_scale, norm2_g))

    per_seq = lambda a: a.reshape(b, lp, a.shape[-1])
    h = x
    for layer in range(depth):
        last = layer == depth - 1
        q, k, v, p, gsb = _input_stage(
            h if layer == 0 else h.reshape(b * lp, d), meta, norm1_g, w_in, b_gate, pool_mix,
            pool_scale, w_branch_pool, layer=layer, tm=tm, tiles_per_seq=tiles_per_seq)
        s = _attention(per_seq(q), per_seq(k), per_seq(v))
        h = _output_stage(
            h, meta, s, per_seq(p), per_seq(gsb), w_branch_sb, w_out, norm2_g, w_ffn_in,
            w_ffn_out, final_norm_g[None], layer=layer, final_norm=last,
            tm=FINAL_TILE if last else tm, row0=N_META if last else 0)

    return h
```

```python
import functools

import numpy as np

import jax
import jax.numpy as jnp
from jax import lax
from jax.experimental import pallas as pl
from jax.experimental.pallas import tpu as pltpu

D_MODEL = 1024
N_META = 16
SB_HEAD_DIM = 64
SB_WIDTH = 512
POOL_WINDOWS = (2, 4, 8, 16)
POOL_WIDTH = 512
POOL_GROUP_DIM = 128
N_IN = 4096
D_FF = 2816
RMS_EPS = 1e-6

LANES = 128
BF16_ROWS = 16
Q_BLOCK = 128
K_BLOCK = 256
ATTN_SLOTS = 8
HALO = 16
FINAL_TILE = 512
FF_CHUNK = 256
COL_CHUNK = 512
VMEM_LIMIT_BYTES = 56 * 1024 * 1024
MASKED_LOGIT = -1e30

BF16 = jnp.bfloat16
F32 = jnp.float32


def _dot(a, b):
    return jnp.dot(a, b, preferred_element_type=F32)


def _rms_norm_rows(x, g):
    ms = jnp.mean(x * x, axis=-1, keepdims=True)
    return x * lax.rsqrt(ms + RMS_EPS) * g


def _sigmoid(x):
    return 1.0 / (1.0 + jnp.exp(-x))


def _input_stage_kernel(tiles_per_seq, frames_only, x_ref, meta_ref, g_ref, w_ref, bg_ref, mix_ref,
                        ps_ref, wbp_ref, q_ref, k_ref, v_ref, p_ref, gsb_ref,
                        h_ref, ubuf_ref, a_ref):
    tm = x_ref.shape[0]
    tile_in_seq = pl.program_id(0) % tiles_per_seq
    n_meta = meta_ref.shape[0]

    @pl.when(tile_in_seq == 0)
    def _():
        ubuf_ref[0:HALO, :] = jnp.zeros((HALO, POOL_WIDTH), F32)

    @pl.when(tile_in_seq != 0)
    def _():
        ubuf_ref[0:HALO, :] = ubuf_ref[tm:tm + HALO, :]

    if frames_only:
        first = tile_in_seq == 0
        head = jnp.where(first, meta_ref[...], x_ref[0:n_meta, :])
        rest = jnp.where(first, x_ref[0:tm - n_meta, :], x_ref[n_meta:tm, :])
        h_ref[0:n_meta, :] = _rms_norm_rows(head, g_ref[...]).astype(BF16)
        h_ref[n_meta:tm, :] = _rms_norm_rows(rest, g_ref[...]).astype(BF16)
    else:
        h_ref[...] = _rms_norm_rows(x_ref[...], g_ref[...]).astype(BF16)

    h = h_ref[...]
    ubuf_ref[HALO:HALO + tm, :] = _dot(h, w_ref[:, 3 * SB_WIDTH:3 * SB_WIDTH + POOL_WIDTH])

    def gate(branch_ref, c):
        first = D_MODEL * (branch_ref is gsb_ref) + c * COL_CHUNK
        w0 = 3 * SB_WIDTH + POOL_WIDTH + first
        branch_ref[:, c * COL_CHUNK:(c + 1) * COL_CHUNK] = _sigmoid(
            _dot(h, w_ref[:, w0:w0 + COL_CHUNK]) + bg_ref[:, first:first + COL_CHUNK])

    gates = [(ref, c) for c in range(D_MODEL // COL_CHUNK) for ref in (p_ref, gsb_ref)]
    pos = tile_in_seq * tm + lax.broadcasted_iota(jnp.int32, (tm, 1), 0)
    for g, window in enumerate(POOL_WINDOWS):
        gate(*gates[g])
        cols = slice(g * POOL_GROUP_DIM, (g + 1) * POOL_GROUP_DIM)
        ug = ubuf_ref[HALO:HALO + tm, cols]
        total = ug
        for d in range(1, window):
            total = total + ubuf_ref[HALO - d:HALO - d + tm, cols]
        cnt = jnp.minimum(pos + 1, window).astype(F32)
        diff = (total / cnt - ug).astype(BF16)
        a_ref[:, cols] = (_dot(diff, mix_ref[g]) * ps_ref[:, cols]).astype(BF16)
    for extra in gates[len(POOL_WINDOWS):]:
        gate(*extra)

    q_ref[...] = (_dot(h, w_ref[:, 0:SB_WIDTH]) * (SB_HEAD_DIM ** -0.5)).astype(BF16)
    k_ref[...] = _dot(h, w_ref[:, SB_WIDTH:2 * SB_WIDTH]).astype(BF16)
    v_ref[...] = _dot(h, w_ref[:, 2 * SB_WIDTH:3 * SB_WIDTH]).astype(BF16)

    a = a_ref[...]
    for c in range(D_MODEL // COL_CHUNK):
        cols = slice(c * COL_CHUNK, (c + 1) * COL_CHUNK)
        p_ref[:, cols] = p_ref[:, cols] * _dot(a, wbp_ref[:, cols])


def _layer_spec(layer, shape, **kwargs):
    return pl.BlockSpec((None,) + shape, lambda *_: (layer,) + (0,) * len(shape), **kwargs)


def _frames_window(tm, n_meta, batch_and_tile):
    def index_map(*grid_idx):
        bi, j = batch_and_tile(*grid_idx)
        return bi, pl.multiple_of(jnp.maximum(j * tm - n_meta, 0), BF16_ROWS), 0
    return pl.BlockSpec((pl.Squeezed(), pl.Element(tm), pl.Element(D_MODEL)), index_map)


def _input_stage(x, meta, g, w, bg, mix, ps, wbp, *, layer, tm, tiles_per_seq):
    frames_only = x.ndim == 3
    t = x.shape[0] * tiles_per_seq * tm if frames_only else x.shape[0]
    full = functools.partial(_layer_spec, layer)
    rows = lambda width: pl.BlockSpec((tm, width), lambda i: (i, 0))
    x_spec = (_frames_window(tm, meta.shape[0], lambda i: (i // tiles_per_seq, i % tiles_per_seq))
              if frames_only else rows(D_MODEL))
    return pl.pallas_call(
        functools.partial(_input_stage_kernel, tiles_per_seq, frames_only),
        grid=(t // tm,),
        in_specs=[x_spec, pl.BlockSpec(meta.shape, lambda i: (0, 0)),
                  full((1, D_MODEL)), full((D_MODEL, N_IN)), full((1, 2 * D_MODEL)),
                  full((len(POOL_WINDOWS), POOL_GROUP_DIM, POOL_GROUP_DIM)), full((1, POOL_WIDTH)),
                  full((POOL_WIDTH, D_MODEL))],
        out_specs=[rows(SB_WIDTH), rows(SB_WIDTH), rows(SB_WIDTH), rows(D_MODEL), rows(D_MODEL)],
        out_shape=[jax.ShapeDtypeStruct((t, SB_WIDTH), BF16)] * 3
                  + [jax.ShapeDtypeStruct((t, D_MODEL), F32)] * 2,
        scratch_shapes=[pltpu.VMEM((tm, D_MODEL), BF16),
                        pltpu.VMEM((tm + HALO, POOL_WIDTH), F32),
                        pltpu.VMEM((tm, POOL_WIDTH), BF16)],
        compiler_params=pltpu.CompilerParams(dimension_semantics=("arbitrary",),
                                             vmem_limit_bytes=VMEM_LIMIT_BYTES),
        name="input_stage",
    )(x, meta, g, w, bg, mix, ps, wbp)


def _attention_schedule(lp):
    kinds = {None: 0}
    blocks = []
    for i in range(lp // Q_BLOCK):
        n = i * Q_BLOCK // K_BLOCK + 1
        for j in range(n):
            if j < n - 1:
                blocks.append((i, j * K_BLOCK // Q_BLOCK, 0))
                continue
            start = min(j * K_BLOCK, lp - K_BLOCK)
            key = (j * K_BLOCK - start, i * Q_BLOCK - start)
            blocks.append((i, start // Q_BLOCK, kinds.setdefault(key, len(kinds))))
    c = np.arange(K_BLOCK)[None, :]
    r = np.arange(Q_BLOCK)[:, None]
    mask = np.zeros((len(kinds), Q_BLOCK, K_BLOCK), np.float32)
    for key, kind in kinds.items():
        if key is not None:
            lo, off = key
            mask[kind] = np.where((c >= lo) & (c < r + off), 0.0, MASKED_LOGIT)
    return np.asarray(blocks, np.int32), mask


def _attention_kernel(n_blocks, sched_ref, q_ref, k_ref, v_ref, tri_ref, mask_ref, eye_ref, o_ref,
                      qs_ref, kt_ref, vc_ref, acc_ref, z_ref, sp_ref, incl_ref, w_ref):
    n_q = qs_ref.shape[0]
    seq_len = q_ref.shape[1]
    lane = lax.broadcasted_iota(jnp.int32, (Q_BLOCK, LANES), 1)

    def chunk(ref, i):
        n_rows = min(Q_BLOCK, seq_len - i * Q_BLOCK)
        rows = ref[0, i * Q_BLOCK:i * Q_BLOCK + n_rows, :]
        if n_rows < Q_BLOCK:
            rows = jnp.concatenate([rows, jnp.zeros((Q_BLOCK - n_rows, LANES), rows.dtype)], axis=0)
        return rows

    for i in range(n_q):
        q2 = chunk(q_ref, i)
        zero = jnp.zeros_like(q2)
        qs_ref[i, 0:Q_BLOCK, :] = jnp.where(lane < SB_HEAD_DIM, q2, zero)
        qs_ref[i, Q_BLOCK:2 * Q_BLOCK, :] = jnp.where(lane >= SB_HEAD_DIM, q2, zero)
        kt_ref[i] = chunk(k_ref, i).T
        vc_ref[i] = chunk(v_ref, i)
    acc_ref[...] = jnp.zeros_like(acc_ref)

    def block(t):
        return sched_ref[t], sched_ref[n_blocks + t], sched_ref[2 * n_blocks + t]

    def raw_scores(t, slot):
        i, c, kind = block(t)
        lhs = jnp.concatenate([qs_ref[i], eye_ref[...]], axis=1)
        keys_t = jnp.concatenate([kt_ref[c], kt_ref[c + 1]], axis=1)
        z_ref[slot] = _dot(lhs, jnp.concatenate([keys_t, mask_ref[kind]], axis=0))

    def softplus(t, slot):
        z = z_ref[slot].astype(BF16)
        sp_ref[slot] = jnp.maximum(z, 0.0) + jnp.log(1.0 + jnp.exp(-jnp.abs(z)))

    def suffix_sums(t, slot):
        incl_ref[slot] = _dot(sp_ref[slot], tri_ref[...])

    def weights(t, slot):
        w_ref[slot] = jnp.exp(z_ref[slot] + incl_ref[slot]).astype(BF16)

    def accumulate(t, slot):
        i, c, _ = block(t)
        decay = jnp.exp(incl_ref[slot, :, 0:1])
        values = jnp.concatenate([vc_ref[c], vc_ref[c + 1]], axis=0)
        acc_ref[i] = acc_ref[i] * decay + _dot(w_ref[slot], values)

    stages = (raw_scores, softplus, suffix_sums, weights, accumulate)
    depth = len(stages) - 1

    def trip(t, t_mod_slots):
        for s in range(depth, -1, -1):
            if isinstance(t, int) and not 0 <= t - s < n_blocks:
                continue
            stages[s](t - s, (t_mod_slots - s) % ATTN_SLOTS)

    for t in range(depth):
        trip(t, t % ATTN_SLOTS)

    n_groups = (n_blocks - depth) // ATTN_SLOTS

    def body(g, c):
        t0 = depth + g * ATTN_SLOTS
        for u in range(ATTN_SLOTS):
            trip(t0 + u, (depth + u) % ATTN_SLOTS)
        return c

    lax.fori_loop(0, n_groups, body, 0)

    for t in range(depth + n_groups * ATTN_SLOTS, n_blocks + depth):
        trip(t, t % ATTN_SLOTS)

    for i in range(n_q):
        n_rows = min(Q_BLOCK, seq_len - i * Q_BLOCK)
        acc = acc_ref[i]
        both = jnp.where(lane < SB_HEAD_DIM, acc[:Q_BLOCK], acc[Q_BLOCK:]).astype(o_ref.dtype)
        o_ref[0, i * Q_BLOCK:i * Q_BLOCK + n_rows, :] = both[:n_rows]


def _attention(q, k, v):
    b, lp, _ = q.shape
    n_pairs = SB_WIDTH // LANES
    assert lp % BF16_ROWS == 0 and lp >= K_BLOCK, lp
    n_q = -(-lp // Q_BLOCK)
    blocks, mask = _attention_schedule(n_q * Q_BLOCK)
    n_blocks = blocks.shape[0]
    sched = jnp.asarray(blocks.T.reshape(-1))
    j = np.arange(K_BLOCK)[:, None]
    s = np.arange(K_BLOCK)[None, :]
    tri = jnp.asarray(np.where(j >= s, -1.0, 0.0), BF16)
    eye = jnp.asarray(np.tile(np.eye(Q_BLOCK), (2, 1)), BF16)
    mask = jnp.asarray(mask, BF16)
    seq = lambda: pl.BlockSpec((1, lp, LANES), lambda bi, hp, sched: (bi, 0, hp))
    return pl.pallas_call(
        functools.partial(_attention_kernel, n_blocks),
        grid_spec=pltpu.PrefetchScalarGridSpec(
            num_scalar_prefetch=1,
            grid=(b, n_pairs),
            in_specs=[seq(), seq(), seq(),
                      pl.BlockSpec(tri.shape, lambda bi, hp, sched: (0, 0)),
                      pl.BlockSpec(mask.shape, lambda bi, hp, sched: (0, 0, 0)),
                      pl.BlockSpec(eye.shape, lambda bi, hp, sched: (0, 0))],
            out_specs=seq(),
            scratch_shapes=[pltpu.VMEM((n_q, 2 * Q_BLOCK, LANES), BF16),
                            pltpu.VMEM((n_q, LANES, Q_BLOCK), BF16),
                            pltpu.VMEM((n_q, Q_BLOCK, LANES), BF16),
                            pltpu.VMEM((n_q, 2 * Q_BLOCK, LANES), F32),
                            pltpu.VMEM((ATTN_SLOTS, 2 * Q_BLOCK, K_BLOCK), F32),
                            pltpu.VMEM((ATTN_SLOTS, 2 * Q_BLOCK, K_BLOCK), BF16),
                            pltpu.VMEM((ATTN_SLOTS, 2 * Q_BLOCK, K_BLOCK), F32),
                            pltpu.VMEM((ATTN_SLOTS, 2 * Q_BLOCK, K_BLOCK), BF16)]),
        out_shape=jax.ShapeDtypeStruct((b, lp, SB_WIDTH), BF16),
        compiler_params=pltpu.CompilerParams(dimension_semantics=("arbitrary", "arbitrary")),
        name="stick_breaking_attention",
    )(sched, q, k, v, tri, mask, eye)


def _output_stage_kernel(final_norm, frames_only, x_ref, meta_ref, s_ref, p_ref, gsb_ref, wbs_ref,
                         wo_ref, g2_ref, wfi_ref, wfo_ref, gf_ref, o_ref,
                         m_ref, x1_ref, h2_ref, act_ref):
    tm = x1_ref.shape[0]
    n_meta = meta_ref.shape[0]
    s = s_ref[...]
    for c in range(D_MODEL // COL_CHUNK):
        cols = slice(c * COL_CHUNK, (c + 1) * COL_CHUNK)
        m_ref[:, cols] = (p_ref[:, cols] + gsb_ref[:, cols] * _dot(s, wbs_ref[:, cols])).astype(BF16)
    m = m_ref[...]
    for c in range(D_MODEL // COL_CHUNK):
        cols = slice(c * COL_CHUNK, (c + 1) * COL_CHUNK)
        merged = _dot(m, wo_ref[:, cols])
        if frames_only:
            first = pl.program_id(1) == 0
            head = jnp.where(first, meta_ref[:, cols], x_ref[0:n_meta, cols])
            rest = jnp.where(first, x_ref[0:tm - n_meta, cols], x_ref[n_meta:tm, cols])
            x1_ref[0:n_meta, cols] = head + merged[0:n_meta]
            x1_ref[n_meta:tm, cols] = rest + merged[n_meta:tm]
        else:
            x1_ref[:, cols] = x_ref[:, cols] + merged
    h2_ref[...] = _rms_norm_rows(x1_ref[...], g2_ref[...]).astype(BF16)
    h2 = h2_ref[...]
    for c in range(D_FF // FF_CHUNK):
        cols = slice(c * FF_CHUNK, (c + 1) * FF_CHUNK)
        gt = _dot(h2, wfi_ref[:, cols])
        up = _dot(h2, wfi_ref[:, D_FF + c * FF_CHUNK:D_FF + (c + 1) * FF_CHUNK])
        act_ref[:, cols] = (gt * _sigmoid(gt) * up).astype(BF16)
    act = act_ref[...]
    for c in range(D_MODEL // COL_CHUNK):
        cols = slice(c * COL_CHUNK, (c + 1) * COL_CHUNK)
        y = x1_ref[:, cols] + _dot(act, wfo_ref[:, cols])
        if final_norm:
            x1_ref[:, cols] = y
        else:
            o_ref[:, cols] = y
    if final_norm:
        o_ref[...] = _rms_norm_rows(x1_ref[...], gf_ref[...])


def _output_stage(x, meta, s, p, gsb, wbs, wo, g2, wfi, wfo, gf, *, layer, tm, row0, final_norm):
    b, lp, _ = s.shape
    frames_only = x.shape[1] != lp
    assert not (frames_only and row0), "frames-only residual input needs whole-sequence tiles"
    n_tiles = (lp - row0) // tm
    full = functools.partial(_layer_spec, layer, pipeline_mode=pl.Buffered(1))
    if row0 == 0:
        rows = lambda width: pl.BlockSpec((None, tm, width), lambda bi, j: (bi, j, 0))
    else:
        rows = lambda width: pl.BlockSpec((pl.Squeezed(), pl.Element(tm), pl.Element(width)),
                                          lambda bi, j: (bi, pl.multiple_of(row0 + j * tm, BF16_ROWS), 0))
    x_spec = _frames_window(tm, meta.shape[0], lambda bi, j: (bi, j)) if frames_only else rows(D_MODEL)
    return pl.pallas_call(
        functools.partial(_output_stage_kernel, final_norm, frames_only),
        grid=(b, n_tiles),
        in_specs=[x_spec, pl.BlockSpec(meta.shape, lambda bi, j: (0, 0)),
                  rows(SB_WIDTH), rows(D_MODEL), rows(D_MODEL),
                  full((SB_WIDTH, D_MODEL)), full((D_MODEL, D_MODEL)), full((1, D_MODEL)),
                  full((D_MODEL, 2 * D_FF)), full((D_FF, D_MODEL)),
                  pl.BlockSpec((1, D_MODEL), lambda bi, j: (0, 0))],
        out_specs=pl.BlockSpec((None, tm, D_MODEL), lambda bi, j: (bi, j, 0)),
        out_shape=jax.ShapeDtypeStruct((b, n_tiles * tm, D_MODEL), F32),
        scratch_shapes=[pltpu.VMEM((tm, D_MODEL), BF16),
                        pltpu.VMEM((tm, D_MODEL), F32),
                        pltpu.VMEM((tm, D_MODEL), BF16),
                        pltpu.VMEM((tm, D_FF), BF16)],
        compiler_params=pltpu.CompilerParams(dimension_semantics=("arbitrary", "arbitrary"),
                                             vmem_limit_bytes=VMEM_LIMIT_BYTES),
        name="output_stage",
    )(x, meta, s, p, gsb, wbs, wo, g2, wfi, wfo, gf)


def kernel(x, meta_tokens, norm1_g, w_in, b_gate, pool_mix, pool_scale, w_branch_pool, w_branch_sb,
           w_out, norm2_g, w_ffn_in, w_ffn_out, final_norm_g):
    b, seq, d = x.shape
    depth = w_in.shape[0]
    lp = N_META + seq
    tiles_per_seq = 3
    tm = lp // tiles_per_seq
    assert tm * tiles_per_seq == lp and tm % BF16_ROWS == 0 and tm >= HALO, (lp, tm)
    assert seq % FINAL_TILE == 0 and N_META % BF16_ROWS == 0, (seq, N_META)

    assert depth >= 2 and meta_tokens.shape == (N_META, d), (depth, meta_tokens.shape)
    meta = meta_tokens.astype(x.dtype)

    w_in, pool_mix, w_branch_pool, w_branch_sb, w_out, w_ffn_in, w_ffn_out = (
        w.astype(BF16) for w in (w_in, pool_mix, w_branch_pool, w_branch_sb, w_out, w_ffn_in, w_ffn_out))
    norm1_g, b_gate, pool_scale, norm2_g = (
        a[:, None, :] for a in (norm1_g, b_gate, pool_scale, norm2_g))

    per_seq = lambda a: a.reshape(b, lp, a.shape[-1])
    h = x
    for layer in range(depth):
        last = layer == depth - 1
        q, k, v, p, gsb = _input_stage(
            h if layer == 0 else h.reshape(b * lp, d), meta, norm1_g, w_in, b_gate, pool_mix,
            pool_scale, w_branch_pool, layer=layer, tm=tm, tiles_per_seq=tiles_per_seq)
        s = _attention(per_seq(q), per_seq(k), per_seq(v))
        h = _output_stage(
            h, meta, s, per_seq(p), per_seq(gsb), w_branch_sb, w_out, norm2_g, w_ffn_in,
            w_ffn_out, final_norm_g[None], layer=layer, final_norm=last,
            tm=FINAL_TILE if last else tm, row0=N_META if last else 0)

    return h
```

```python
import functools

import numpy as np

import jax
import jax.numpy as jnp
from jax import lax
from jax.experimental import pallas as pl
from jax.experimental.pallas import tpu as pltpu

D_MODEL = 1024
N_META = 16
SB_HEAD_DIM = 64
SB_WIDTH = 512
POOL_WINDOWS = (2, 4, 8, 16)
POOL_WIDTH = 512
POOL_GROUP_DIM = 128
N_IN = 4096
D_FF = 2816
RMS_EPS = 1e-6

LANES = 128
BF16_ROWS = 16
Q_BLOCK = 128
K_BLOCK = 256
ATTN_SLOTS = 11
HALO = 16
FINAL_TILE = 512
FF_CHUNK = 256
COL_CHUNK = 512
VMEM_LIMIT_BYTES = 56 * 1024 * 1024
MASKED_LOGIT = -1e30

BF16 = jnp.bfloat16
F32 = jnp.float32


def _dot(a, b):
    return jnp.dot(a, b, preferred_element_type=F32)


def _rms_norm_rows(x, g):
    ms = jnp.mean(x * x, axis=-1, keepdims=True)
    return x * lax.rsqrt(ms + RMS_EPS) * g


def _sigmoid(x):
    return 1.0 / (1.0 + jnp.exp(-x))


def _input_stage_kernel(tiles_per_seq, frames_only, x_ref, meta_ref, g_ref, w_ref, bg_ref, mix_ref,
                        ps_ref, wbp_ref, q_ref, k_ref, v_ref, p_ref, gsb_ref,
                        h_ref, ubuf_ref, a_ref):
    tm = x_ref.shape[0]
    tile_in_seq = pl.program_id(0) % tiles_per_seq
    n_meta = meta_ref.shape[0]

    @pl.when(tile_in_seq == 0)
    def _():
        ubuf_ref[0:HALO, :] = jnp.zeros((HALO, POOL_WIDTH), F32)

    @pl.when(tile_in_seq != 0)
    def _():
        ubuf_ref[0:HALO, :] = ubuf_ref[tm:tm + HALO, :]

    if frames_only:
        first = tile_in_seq == 0
        head = jnp.where(first, meta_ref[...], x_ref[0:n_meta, :])
        rest = jnp.where(first, x_ref[0:tm - n_meta, :], x_ref[n_meta:tm, :])
        h_ref[0:n_meta, :] = _rms_norm_rows(head, g_ref[...]).astype(BF16)
        h_ref[n_meta:tm, :] = _rms_norm_rows(rest, g_ref[...]).astype(BF16)
    else:
        h_ref[...] = _rms_norm_rows(x_ref[...], g_ref[...]).astype(BF16)

    h = h_ref[...]
    ubuf_ref[HALO:HALO + tm, :] = _dot(h, w_ref[:, 3 * SB_WIDTH:3 * SB_WIDTH + POOL_WIDTH])

    def gate(branch_ref, c):
        first = D_MODEL * (branch_ref is gsb_ref) + c * COL_CHUNK
        w0 = 3 * SB_WIDTH + POOL_WIDTH + first
        branch_ref[:, c * COL_CHUNK:(c + 1) * COL_CHUNK] = _sigmoid(
            _dot(h, w_ref[:, w0:w0 + COL_CHUNK]) + bg_ref[:, first:first + COL_CHUNK])

    gates = [(ref, c) for c in range(D_MODEL // COL_CHUNK) for ref in (p_ref, gsb_ref)]
    pos = tile_in_seq * tm + lax.broadcasted_iota(jnp.int32, (tm, 1), 0)
    for g, window in enumerate(POOL_WINDOWS):
        gate(*gates[g])
        cols = slice(g * POOL_GROUP_DIM, (g + 1) * POOL_GROUP_DIM)
        ug = ubuf_ref[HALO:HALO + tm, cols]
        total = ug
        for d in range(1, window):
            total = total + ubuf_ref[HALO - d:HALO - d + tm, cols]
        cnt = jnp.minimum(pos + 1, window).astype(F32)
        diff = (total / cnt - ug).astype(BF16)
        a_ref[:, cols] = (_dot(diff, mix_ref[g]) * ps_ref[:, cols]).astype(BF16)
    for extra in gates[len(POOL_WINDOWS):]:
        gate(*extra)

    q_ref[...] = (_dot(h, w_ref[:, 0:SB_WIDTH]) * (SB_HEAD_DIM ** -0.5)).astype(BF16)
    k_ref[...] = _dot(h, w_ref[:, SB_WIDTH:2 * SB_WIDTH]).astype(BF16)
    v_ref[...] = _dot(h, w_ref[:, 2 * SB_WIDTH:3 * SB_WIDTH]).astype(BF16)

    a = a_ref[...]
    for c in range(D_MODEL // COL_CHUNK):
        cols = slice(c * COL_CHUNK, (c + 1) * COL_CHUNK)
        p_ref[:, cols] = p_ref[:, cols] * _dot(a, wbp_ref[:, cols])


def _layer_spec(layer, shape, **kwargs):
    return pl.BlockSpec((None,) + shape, lambda *_: (layer,) + (0,) * len(shape), **kwargs)


def _frames_window(tm, n_meta, batch_and_tile):
    def index_map(*grid_idx):
        bi, j = batch_and_tile(*grid_idx)
        return bi, pl.multiple_of(jnp.maximum(j * tm - n_meta, 0), BF16_ROWS), 0
    return pl.BlockSpec((pl.Squeezed(), pl.Element(tm), pl.Element(D_MODEL)), index_map)


def _input_stage(x, meta, g, w, bg, mix, ps, wbp, *, layer, tm, tiles_per_seq):
    frames_only = x.ndim == 3
    t = x.shape[0] * tiles_per_seq * tm if frames_only else x.shape[0]
    full = functools.partial(_layer_spec, layer)
    rows = lambda width: pl.BlockSpec((tm, width), lambda i: (i, 0))
    x_spec = (_frames_window(tm, meta.shape[0], lambda i: (i // tiles_per_seq, i % tiles_per_seq))
              if frames_only else rows(D_MODEL))
    return pl.pallas_call(
        functools.partial(_input_stage_kernel, tiles_per_seq, frames_only),
        grid=(t // tm,),
        in_specs=[x_spec, pl.BlockSpec(meta.shape, lambda i: (0, 0)),
                  full((1, D_MODEL)), full((D_MODEL, N_IN)), full((1, 2 * D_MODEL)),
                  full((len(POOL_WINDOWS), POOL_GROUP_DIM, POOL_GROUP_DIM)), full((1, POOL_WIDTH)),
                  full((POOL_WIDTH, D_MODEL))],
        out_specs=[rows(SB_WIDTH), rows(SB_WIDTH), rows(SB_WIDTH), rows(D_MODEL), rows(D_MODEL)],
        out_shape=[jax.ShapeDtypeStruct((t, SB_WIDTH), BF16)] * 3
                  + [jax.ShapeDtypeStruct((t, D_MODEL), F32)] * 2,
        scratch_shapes=[pltpu.VMEM((tm, D_MODEL), BF16),
                        pltpu.VMEM((tm + HALO, POOL_WIDTH), F32),
                        pltpu.VMEM((tm, POOL_WIDTH), BF16)],
        compiler_params=pltpu.CompilerParams(dimension_semantics=("arbitrary",),
                                             vmem_limit_bytes=VMEM_LIMIT_BYTES),
        name="input_stage",
    )(x, meta, g, w, bg, mix, ps, wbp)


def _attention_schedule(lp):
    kinds = {None: 0}
    blocks = []
    for i in range(lp // Q_BLOCK):
        n = i * Q_BLOCK // K_BLOCK + 1
        for j in range(n):
            if j < n - 1:
                blocks.append((i, j * K_BLOCK // Q_BLOCK, 0))
                continue
            start = min(j * K_BLOCK, lp - K_BLOCK)
            key = (j * K_BLOCK - start, i * Q_BLOCK - start)
            blocks.append((i, start // Q_BLOCK, kinds.setdefault(key, len(kinds))))
    c = np.arange(K_BLOCK)[None, :]
    r = np.arange(Q_BLOCK)[:, None]
    mask = np.zeros((len(kinds), Q_BLOCK, K_BLOCK), np.float32)
    for key, kind in kinds.items():
        if key is not None:
            lo, off = key
            mask[kind] = np.where((c >= lo) & (c < r + off), 0.0, MASKED_LOGIT)
    return np.asarray(blocks, np.int32), mask


def _attention_kernel(n_blocks, sched_ref, q_ref, k_ref, v_ref, tri_ref, mask_ref, eye_ref, o_ref,
                      qs_ref, kt_ref, vc_ref, acc_ref, z_ref, sp_ref, incl_ref, w_ref):
    n_q = qs_ref.shape[0]
    seq_len = q_ref.shape[1]
    lane = lax.broadcasted_iota(jnp.int32, (Q_BLOCK, LANES), 1)

    def chunk(ref, i):
        n_rows = min(Q_BLOCK, seq_len - i * Q_BLOCK)
        rows = ref[0, i * Q_BLOCK:i * Q_BLOCK + n_rows, :]
        if n_rows < Q_BLOCK:
            rows = jnp.concatenate([rows, jnp.zeros((Q_BLOCK - n_rows, LANES), rows.dtype)], axis=0)
        return rows

    for i in range(n_q):
        q2 = chunk(q_ref, i)
        zero = jnp.zeros_like(q2)
        qs_ref[i, 0:Q_BLOCK, :] = jnp.where(lane < SB_HEAD_DIM, q2, zero)
        qs_ref[i, Q_BLOCK:2 * Q_BLOCK, :] = jnp.where(lane >= SB_HEAD_DIM, q2, zero)
        kt_ref[i] = chunk(k_ref, i).T
        vc_ref[i] = chunk(v_ref, i)
    acc_ref[...] = jnp.zeros_like(acc_ref)

    def block(t):
        return sched_ref[t], sched_ref[n_blocks + t], sched_ref[2 * n_blocks + t]

    def raw_scores(t, slot):
        i, c, kind = block(t)
        lhs = jnp.concatenate([qs_ref[i], eye_ref[...]], axis=1)
        keys_t = jnp.concatenate([kt_ref[c], kt_ref[c + 1]], axis=1)
        z_ref[slot] = _dot(lhs, jnp.concatenate([keys_t, mask_ref[kind]], axis=0))

    def softplus(t, slot):
        z = z_ref[slot].astype(BF16)
        sp_ref[slot] = jnp.maximum(z, 0.0) + jnp.log(1.0 + jnp.exp(-jnp.abs(z)))

    def suffix_sums(t, slot):
        incl_ref[slot] = _dot(sp_ref[slot], tri_ref[...])

    def weights(t, slot):
        w_ref[slot] = jnp.exp(z_ref[slot] + incl_ref[slot]).astype(BF16)

    def accumulate(t, slot):
        i, c, _ = block(t)
        decay = jnp.exp(incl_ref[slot, :, 0:1])
        values = jnp.concatenate([vc_ref[c], vc_ref[c + 1]], axis=0)
        acc_ref[i] = acc_ref[i] * decay + _dot(w_ref[slot], values)

    stages = (raw_scores, softplus, suffix_sums, weights, accumulate)
    depth = len(stages) - 1

    def trip(t, t_mod_slots):
        for s in range(depth, -1, -1):
            if isinstance(t, int) and not 0 <= t - s < n_blocks:
                continue
            stages[s](t - s, (t_mod_slots - s) % ATTN_SLOTS)

    for t in range(depth):
        trip(t, t % ATTN_SLOTS)

    n_groups = (n_blocks - depth) // ATTN_SLOTS

    def body(g, c):
        t0 = depth + g * ATTN_SLOTS
        for u in range(ATTN_SLOTS):
            trip(t0 + u, (depth + u) % ATTN_SLOTS)
        return c

    lax.fori_loop(0, n_groups, body, 0)

    for t in range(depth + n_groups * ATTN_SLOTS, n_blocks + depth):
        trip(t, t % ATTN_SLOTS)

    for i in range(n_q):
        n_rows = min(Q_BLOCK, seq_len - i * Q_BLOCK)
        acc = acc_ref[i]
        both = jnp.where(lane < SB_HEAD_DIM, acc[:Q_BLOCK], acc[Q_BLOCK:]).astype(o_ref.dtype)
        o_ref[0, i * Q_BLOCK:i * Q_BLOCK + n_rows, :] = both[:n_rows]


def _attention(q, k, v):
    b, lp, _ = q.shape
    n_pairs = SB_WIDTH // LANES
    assert lp % BF16_ROWS == 0 and lp >= K_BLOCK, lp
    n_q = -(-lp // Q_BLOCK)
    blocks, mask = _attention_schedule(n_q * Q_BLOCK)
    n_blocks = blocks.shape[0]
    sched = jnp.asarray(blocks.T.reshape(-1))
    j = np.arange(K_BLOCK)[:, None]
    s = np.arange(K_BLOCK)[None, :]
    tri = jnp.asarray(np.where(j >= s, -1.0, 0.0), BF16)
    eye = jnp.asarray(np.tile(np.eye(Q_BLOCK), (2, 1)), BF16)
    mask = jnp.asarray(mask, BF16)
    seq = lambda: pl.BlockSpec((1, lp, LANES), lambda bi, hp, sched: (bi, 0, hp))
    return pl.pallas_call(
        functools.partial(_attention_kernel, n_blocks),
        grid_spec=pltpu.PrefetchScalarGridSpec(
            num_scalar_prefetch=1,
            grid=(b, n_pairs),
            in_specs=[seq(), seq(), seq(),
                      pl.BlockSpec(tri.shape, lambda bi, hp, sched: (0, 0)),
                      pl.BlockSpec(mask.shape, lambda bi, hp, sched: (0, 0, 0)),
                      pl.BlockSpec(eye.shape, lambda bi, hp, sched: (0, 0))],
            out_specs=seq(),
            scratch_shapes=[pltpu.VMEM((n_q, 2 * Q_BLOCK, LANES), BF16),
                            pltpu.VMEM((n_q, LANES, Q_BLOCK), BF16),
                            pltpu.VMEM((n_q, Q_BLOCK, LANES), BF16),
                            pltpu.VMEM((n_q, 2 * Q_BLOCK, LANES), F32),
                            pltpu.VMEM((ATTN_SLOTS, 2 * Q_BLOCK, K_BLOCK), F32),
                            pltpu.VMEM((ATTN_SLOTS, 2 * Q_BLOCK, K_BLOCK), BF16),
                            pltpu.VMEM((ATTN_SLOTS, 2 * Q_BLOCK, K_BLOCK), F32),
                            pltpu.VMEM((ATTN_SLOTS, 2 * Q_BLOCK, K_BLOCK), BF16)]),
        out_shape=jax.ShapeDtypeStruct((b, lp, SB_WIDTH), BF16),
        compiler_params=pltpu.CompilerParams(dimension_semantics=("arbitrary", "arbitrary")),
        name="stick_breaking_attention",
    )(sched, q, k, v, tri, mask, eye)


def _output_stage_kernel(final_norm, frames_only, x_ref, meta_ref, s_ref, p_ref, gsb_ref, wbs_ref,
                         wo_ref, g2_ref, wfi_ref, wfo_ref, gf_ref, o_ref,
                         m_ref, x1_ref, h2_ref, act_ref):
    tm = x1_ref.shape[0]
    n_meta = meta_ref.shape[0]
    s = s_ref[...]
    for c in range(D_MODEL // COL_CHUNK):
        cols = slice(c * COL_CHUNK, (c + 1) * COL_CHUNK)
        m_ref[:, cols] = (p_ref[:, cols] + gsb_ref[:, cols] * _dot(s, wbs_ref[:, cols])).astype(BF16)
    m = m_ref[...]
    for c in range(D_MODEL // COL_CHUNK):
        cols = slice(c * COL_CHUNK, (c + 1) * COL_CHUNK)
        merged = _dot(m, wo_ref[:, cols])
        if frames_only:
            first = pl.program_id(1) == 0
            head = jnp.where(first, meta_ref[:, cols], x_ref[0:n_meta, cols])
            rest = jnp.where(first, x_ref[0:tm - n_meta, cols], x_ref[n_meta:tm, cols])
            x1_ref[0:n_meta, cols] = head + merged[0:n_meta]
            x1_ref[n_meta:tm, cols] = rest + merged[n_meta:tm]
        else:
            x1_ref[:, cols] = x_ref[:, cols] + merged
    h2_ref[...] = _rms_norm_rows(x1_ref[...], g2_ref[...]).astype(BF16)
    h2 = h2_ref[...]
    for c in range(D_FF // FF_CHUNK):
        cols = slice(c * FF_CHUNK, (c + 1) * FF_CHUNK)
        gt = _dot(h2, wfi_ref[:, cols])
        up = _dot(h2, wfi_ref[:, D_FF + c * FF_CHUNK:D_FF + (c + 1) * FF_CHUNK])
        act_ref[:, cols] = (gt * _sigmoid(gt) * up).astype(BF16)
    act = act_ref[...]
    for c in range(D_MODEL // COL_CHUNK):
        cols = slice(c * COL_CHUNK, (c + 1) * COL_CHUNK)
        y = x1_ref[:, cols] + _dot(act, wfo_ref[:, cols])
        if final_norm:
            x1_ref[:, cols] = y
        else:
            o_ref[:, cols] = y
    if final_norm:
        o_ref[...] = _rms_norm_rows(x1_ref[...], gf_ref[...])


def _output_stage(x, meta, s, p, gsb, wbs, wo, g2, wfi, wfo, gf, *, layer, tm, row0, final_norm):
    b, lp, _ = s.shape
    frames_only = x.shape[1] != lp
    assert not (frames_only and row0), "frames-only residual input needs whole-sequence tiles"
    n_tiles = (lp - row0) // tm
    full = functools.partial(_layer_spec, layer, pipeline_mode=pl.Buffered(1))
    if row0 == 0:
        rows = lambda width: pl.BlockSpec((None, tm, width), lambda bi, j: (bi, j, 0))
    else:
        rows = lambda width: pl.BlockSpec((pl.Squeezed(), pl.Element(tm), pl.Element(width)),
                                          lambda bi, j: (bi, pl.multiple_of(row0 + j * tm, BF16_ROWS), 0))
    x_spec = _frames_window(tm, meta.shape[0], lambda bi, j: (bi, j)) if frames_only else rows(D_MODEL)
    return pl.pallas_call(
        functools.partial(_output_stage_kernel, final_norm, frames_only),
        grid=(b, n_tiles),
        in_specs=[x_spec, pl.BlockSpec(meta.shape, lambda bi, j: (0, 0)),
                  rows(SB_WIDTH), rows(D_MODEL), rows(D_MODEL),
                  full((SB_WIDTH, D_MODEL)), full((D_MODEL, D_MODEL)), full((1, D_MODEL)),
                  full((D_MODEL, 2 * D_FF)), full((D_FF, D_MODEL)),
                  pl.BlockSpec((1, D_MODEL), lambda bi, j: (0, 0))],
        out_specs=pl.BlockSpec((None, tm, D_MODEL), lambda bi, j: (bi, j, 0)),
        out_shape=jax.ShapeDtypeStruct((b, n_tiles * tm, D_MODEL), F32),
        scratch_shapes=[pltpu.VMEM((tm, D_MODEL), BF16),
                        pltpu.VMEM((tm, D_MODEL), F32),
                        pltpu.VMEM((tm, D_MODEL), BF16),
                        pltpu.VMEM((tm, D_FF), BF16)],
        compiler_params=pltpu.CompilerParams(dimension_semantics=("arbitrary", "arbitrary"),
                                             vmem_limit_bytes=VMEM_LIMIT_BYTES),
        name="output_stage",
    )(x, meta, s, p, gsb, wbs, wo, g2, wfi, wfo, gf)


def kernel(x, meta_tokens, norm1_g, w_in, b_gate, pool_mix, pool_scale, w_branch_pool, w_branch_sb,
           w_out, norm2_g, w_ffn_in, w_ffn_out, final_norm_g):
    b, seq, d = x.shape
    depth = w_in.shape[0]
    lp = N_META + seq
    tiles_per_seq = 3
    tm = lp // tiles_per_seq
    assert tm * tiles_per_seq == lp and tm % BF16_ROWS == 0 and tm >= HALO, (lp, tm)
    assert seq % FINAL_TILE == 0 and N_META % BF16_ROWS == 0, (seq, N_META)

    assert depth >= 2 and meta_tokens.shape == (N_META, d), (depth, meta_tokens.shape)
    meta = meta_tokens.astype(x.dtype)

    w_in, pool_mix, w_branch_pool, w_branch_sb, w_out, w_ffn_in, w_ffn_out = (
        w.astype(BF16) for w in (w_in, pool_mix, w_branch_pool, w_branch_sb, w_out, w_ffn_in, w_ffn_out))
    norm1_g, b_gate, pool_scale, norm2_g = (
        a[:, None, :] for a in (norm1_g, b_gate, pool_scale, norm2_g))

    per_seq = lambda a: a.reshape(b, lp, a.shape[-1])
    h = x
    for layer in range(depth):
        last = layer == depth - 1
        q, k, v, p, gsb = _input_stage(
            h if layer == 0 else h.reshape(b * lp, d), meta, norm1_g, w_in, b_gate, pool_mix,
            pool_scale, w_branch_pool, layer=layer, tm=tm, tiles_per_seq=tiles_per_seq)
        s = _attention(per_seq(q), per_seq(k), per_seq(v))
        h = _output_stage(
            h, meta, s, per_seq(p), per_seq(gsb), w_branch_sb, w_out, norm2_g, w_ffn_in,
            w_ffn_out, final_norm_g[None], layer=layer, final_norm=last,
            tm=FINAL_TILE if last else tm, row0=N_META if last else 0)

    return h
```

```python
import functools

import numpy as np

import jax
import jax.numpy as jnp
from jax import lax
from jax.experimental import pallas as pl
from jax.experimental.pallas import tpu as pltpu

D_MODEL = 1024
N_META = 16
SB_HEAD_DIM = 64
SB_WIDTH = 512
POOL_WINDOWS = (2, 4, 8, 16)
POOL_WIDTH = 512
POOL_GROUP_DIM = 128
N_IN = 4096
D_FF = 2816
RMS_EPS = 1e-6

LANES = 128
BF16_ROWS = 16
Q_BLOCK = 128
K_BLOCK = 256
ATTN_SLOTS = 11
HALO = 16
FINAL_TILE = 512
FF_CHUNK = 256
COL_CHUNK = 512
VMEM_LIMIT_BYTES = 56 * 1024 * 1024
MASKED_LOGIT = -1e30

BF16 = jnp.bfloat16
F32 = jnp.float32


def _dot(a, b):
    return jnp.dot(a, b, preferred_element_type=F32)


def _rms_norm_rows(x, g):
    ms = jnp.mean(x * x, axis=-1, keepdims=True)
    return x * lax.rsqrt(ms + RMS_EPS) * g


def _sigmoid(x):
    return 1.0 / (1.0 + jnp.exp(-x))


def _input_stage_kernel(tiles_per_seq, frames_only, x_ref, meta_ref, g_ref, w_ref, bg_ref, mix_ref,
                        ps_ref, wbp_ref, q_ref, k_ref, v_ref, p_ref, gsb_ref,
                        h_ref, ubuf_ref, a_ref):
    tm = x_ref.shape[0]
    tile_in_seq = pl.program_id(0) % tiles_per_seq
    n_meta = meta_ref.shape[0]

    @pl.when(tile_in_seq == 0)
    def _():
        ubuf_ref[0:HALO, :] = jnp.zeros((HALO, POOL_WIDTH), F32)

    @pl.when(tile_in_seq != 0)
    def _():
        ubuf_ref[0:HALO, :] = ubuf_ref[tm:tm + HALO, :]

    if frames_only:
        first = tile_in_seq == 0
        head = jnp.where(first, meta_ref[...], x_ref[0:n_meta, :])
        rest = jnp.where(first, x_ref[0:tm - n_meta, :], x_ref[n_meta:tm, :])
        h_ref[0:n_meta, :] = _rms_norm_rows(head, g_ref[...]).astype(BF16)
        h_ref[n_meta:tm, :] = _rms_norm_rows(rest, g_ref[...]).astype(BF16)
    else:
        h_ref[...] = _rms_norm_rows(x_ref[...], g_ref[...]).astype(BF16)

    h = h_ref[...]
    ubuf_ref[HALO:HALO + tm, :] = _dot(h, w_ref[:, 3 * SB_WIDTH:3 * SB_WIDTH + POOL_WIDTH])

    def gate(branch_ref, c):
        first = D_MODEL * (branch_ref is gsb_ref) + c * COL_CHUNK
        w0 = 3 * SB_WIDTH + POOL_WIDTH + first
        branch_ref[:, c * COL_CHUNK:(c + 1) * COL_CHUNK] = _sigmoid(
            _dot(h, w_ref[:, w0:w0 + COL_CHUNK]) + bg_ref[:, first:first + COL_CHUNK])

    gates = [(ref, c) for c in range(D_MODEL // COL_CHUNK) for ref in (p_ref, gsb_ref)]
    pos = tile_in_seq * tm + lax.broadcasted_iota(jnp.int32, (tm, 1), 0)
    for g, window in enumerate(POOL_WINDOWS):
        gate(*gates[g])
        cols = slice(g * POOL_GROUP_DIM, (g + 1) * POOL_GROUP_DIM)
        ug = ubuf_ref[HALO:HALO + tm, cols]
        total = ug
        for d in range(1, window):
            total = total + ubuf_ref[HALO - d:HALO - d + tm, cols]
        cnt = jnp.minimum(pos + 1, window).astype(F32)
        diff = (total / cnt - ug).astype(BF16)
        a_ref[:, cols] = (_dot(diff, mix_ref[g]) * ps_ref[:, cols]).astype(BF16)
    for extra in gates[len(POOL_WINDOWS):]:
        gate(*extra)

    q_ref[...] = (_dot(h, w_ref[:, 0:SB_WIDTH]) * (SB_HEAD_DIM ** -0.5)).astype(BF16)
    k_ref[...] = _dot(h, w_ref[:, SB_WIDTH:2 * SB_WIDTH]).astype(BF16)
    v_ref[...] = _dot(h, w_ref[:, 2 * SB_WIDTH:3 * SB_WIDTH]).astype(BF16)

    a = a_ref[...]
    for c in range(D_MODEL // COL_CHUNK):
        cols = slice(c * COL_CHUNK, (c + 1) * COL_CHUNK)
        p_ref[:, cols] = p_ref[:, cols] * _dot(a, wbp_ref[:, cols])


def _layer_spec(layer, shape, **kwargs):
    return pl.BlockSpec((None,) + shape, lambda *_: (layer,) + (0,) * len(shape), **kwargs)


def _frames_window(tm, n_meta, batch_and_tile):
    def index_map(*grid_idx):
        bi, j = batch_and_tile(*grid_idx)
        return bi, pl.multiple_of(jnp.maximum(j * tm - n_meta, 0), BF16_ROWS), 0
    return pl.BlockSpec((pl.Squeezed(), pl.Element(tm), pl.Element(D_MODEL)), index_map)


def _input_stage(x, meta, g, w, bg, mix, ps, wbp, *, layer, tm, tiles_per_seq):
    frames_only = x.ndim == 3
    t = x.shape[0] * tiles_per_seq * tm if frames_only else x.shape[0]
    full = functools.partial(_layer_spec, layer)
    rows = lambda width: pl.BlockSpec((tm, width), lambda i: (i, 0))
    x_spec = (_frames_window(tm, meta.shape[0], lambda i: (i // tiles_per_seq, i % tiles_per_seq))
              if frames_only else rows(D_MODEL))
    return pl.pallas_call(
        functools.partial(_input_stage_kernel, tiles_per_seq, frames_only),
        grid=(t // tm,),
        in_specs=[x_spec, pl.BlockSpec(meta.shape, lambda i: (0, 0)),
                  full((1, D_MODEL)), full((D_MODEL, N_IN)), full((1, 2 * D_MODEL)),
                  full((len(POOL_WINDOWS), POOL_GROUP_DIM, POOL_GROUP_DIM)), full((1, POOL_WIDTH)),
                  full((POOL_WIDTH, D_MODEL))],
        out_specs=[rows(SB_WIDTH), rows(SB_WIDTH), rows(SB_WIDTH), rows(D_MODEL), rows(D_MODEL)],
        out_shape=[jax.ShapeDtypeStruct((t, SB_WIDTH), BF16)] * 3
                  + [jax.ShapeDtypeStruct((t, D_MODEL), F32)] * 2,
        scratch_shapes=[pltpu.VMEM((tm, D_MODEL), BF16),
                        pltpu.VMEM((tm + HALO, POOL_WIDTH), F32),
                        pltpu.VMEM((tm, POOL_WIDTH), BF16)],
        compiler_params=pltpu.CompilerParams(dimension_semantics=("arbitrary",),
                                             vmem_limit_bytes=VMEM_LIMIT_BYTES),
        name="input_stage",
    )(x, meta, g, w, bg, mix, ps, wbp)


def _attention_schedule(lp):
    kinds = {None: 0}
    blocks = []
    for i in range(lp // Q_BLOCK):
        n = i * Q_BLOCK // K_BLOCK + 1
        for j in range(n):
            if j < n - 1:
                blocks.append((i, j * K_BLOCK // Q_BLOCK, 0))
                continue
            start = min(j * K_BLOCK, lp - K_BLOCK)
            key = (j * K_BLOCK - start, i * Q_BLOCK - start)
            blocks.append((i, start // Q_BLOCK, kinds.setdefault(key, len(kinds))))
    c = np.arange(K_BLOCK)[None, :]
    r = np.arange(Q_BLOCK)[:, None]
    mask = np.zeros((len(kinds), Q_BLOCK, K_BLOCK), np.float32)
    for key, kind in kinds.items():
        if key is not None:
            lo, off = key
            mask[kind] = np.where((c >= lo) & (c < r + off), 0.0, MASKED_LOGIT)
    return np.asarray(blocks, np.int32), mask


def _attention_kernel(blocks, q_ref, k_ref, v_ref, tri_ref, mask_ref, eye_ref, o_ref,
                      qs_ref, kt_ref, vc_ref, acc_ref, z_ref, sp_ref, incl_ref, w_ref):
    n_blocks = len(blocks)
    n_q = qs_ref.shape[0]
    seq_len = q_ref.shape[1]
    lane = lax.broadcasted_iota(jnp.int32, (Q_BLOCK, LANES), 1)

    def chunk(ref, i):
        n_rows = min(Q_BLOCK, seq_len - i * Q_BLOCK)
        rows = ref[0, i * Q_BLOCK:i * Q_BLOCK + n_rows, :]
        if n_rows < Q_BLOCK:
            rows = jnp.concatenate([rows, jnp.zeros((Q_BLOCK - n_rows, LANES), rows.dtype)], axis=0)
        return rows

    for i in range(n_q):
        q2 = chunk(q_ref, i)
        zero = jnp.zeros_like(q2)
        qs_ref[i, 0:Q_BLOCK, :] = jnp.where(lane < SB_HEAD_DIM, q2, zero)
        qs_ref[i, Q_BLOCK:2 * Q_BLOCK, :] = jnp.where(lane >= SB_HEAD_DIM, q2, zero)
        kt_ref[i] = chunk(k_ref, i).T
        vc_ref[i] = chunk(v_ref, i)
    acc_ref[...] = jnp.zeros_like(acc_ref)

    def block(t):
        return blocks[t]

    def raw_scores(t, slot):
        i, c, kind = block(t)
        lhs = jnp.concatenate([qs_ref[i], eye_ref[...]], axis=1)
        keys_t = jnp.concatenate([kt_ref[c], kt_ref[c + 1]], axis=1)
        z_ref[slot] = _dot(lhs, jnp.concatenate([keys_t, mask_ref[kind]], axis=0))

    def softplus(t, slot):
        z = z_ref[slot].astype(BF16)
        sp_ref[slot] = jnp.maximum(z, 0.0) + jnp.log(1.0 + jnp.exp(-jnp.abs(z)))

    def suffix_sums(t, slot):
        incl_ref[slot] = _dot(sp_ref[slot], tri_ref[...])

    def weights(t, slot):
        w_ref[slot] = jnp.exp(z_ref[slot] + incl_ref[slot]).astype(BF16)

    def accumulate(t, slot):
        i, c, _ = block(t)
        decay = jnp.exp(incl_ref[slot, :, 0:1])
        values = jnp.concatenate([vc_ref[c], vc_ref[c + 1]], axis=0)
        acc_ref[i] = acc_ref[i] * decay + _dot(w_ref[slot], values)

    stages = (raw_scores, softplus, suffix_sums, weights, accumulate)
    depth = len(stages) - 1

    for t in range(n_blocks + depth):
        for s in range(depth, -1, -1):
            if 0 <= t - s < n_blocks:
                stages[s](t - s, (t - s) % ATTN_SLOTS)

    for i in range(n_q):
        n_rows = min(Q_BLOCK, seq_len - i * Q_BLOCK)
        acc = acc_ref[i]
        both = jnp.where(lane < SB_HEAD_DIM, acc[:Q_BLOCK], acc[Q_BLOCK:]).astype(o_ref.dtype)
        o_ref[0, i * Q_BLOCK:i * Q_BLOCK + n_rows, :] = both[:n_rows]


def _attention(q, k, v):
    b, lp, _ = q.shape
    n_pairs = SB_WIDTH // LANES
    assert lp % BF16_ROWS == 0 and lp >= K_BLOCK, lp
    n_q = -(-lp // Q_BLOCK)
    blocks, mask = _attention_schedule(n_q * Q_BLOCK)
    blocks = tuple(tuple(int(v) for v in row) for row in blocks)
    j = np.arange(K_BLOCK)[:, None]
    s = np.arange(K_BLOCK)[None, :]
    tri = jnp.asarray(np.where(j >= s, -1.0, 0.0), BF16)
    eye = jnp.asarray(np.tile(np.eye(Q_BLOCK), (2, 1)), BF16)
    mask = jnp.asarray(mask, BF16)
    seq = lambda: pl.BlockSpec((1, lp, LANES), lambda bi, hp: (bi, 0, hp))
    return pl.pallas_call(
        functools.partial(_attention_kernel, blocks),
        grid=(b, n_pairs),
        in_specs=[seq(), seq(), seq(),
                  pl.BlockSpec(tri.shape, lambda bi, hp: (0, 0)),
                  pl.BlockSpec(mask.shape, lambda bi, hp: (0, 0, 0)),
                  pl.BlockSpec(eye.shape, lambda bi, hp: (0, 0))],
        out_specs=seq(),
        scratch_shapes=[pltpu.VMEM((n_q, 2 * Q_BLOCK, LANES), BF16),
                        pltpu.VMEM((n_q, LANES, Q_BLOCK), BF16),
                        pltpu.VMEM((n_q, Q_BLOCK, LANES), BF16),
                        pltpu.VMEM((n_q, 2 * Q_BLOCK, LANES), F32),
                        pltpu.VMEM((ATTN_SLOTS, 2 * Q_BLOCK, K_BLOCK), F32),
                        pltpu.VMEM((ATTN_SLOTS, 2 * Q_BLOCK, K_BLOCK), BF16),
                        pltpu.VMEM((ATTN_SLOTS, 2 * Q_BLOCK, K_BLOCK), F32),
                        pltpu.VMEM((ATTN_SLOTS, 2 * Q_BLOCK, K_BLOCK), BF16)],
        out_shape=jax.ShapeDtypeStruct((b, lp, SB_WIDTH), BF16),
        compiler_params=pltpu.CompilerParams(dimension_semantics=("arbitrary", "arbitrary")),
        name="stick_breaking_attention",
    )(q, k, v, tri, mask, eye)


def _output_stage_kernel(final_norm, frames_only, x_ref, meta_ref, s_ref, p_ref, gsb_ref, wbs_ref,
                         wo_ref, g2_ref, wfi_ref, wfo_ref, gf_ref, o_ref,
                         m_ref, x1_ref, h2_ref, act_ref):
    tm = x1_ref.shape[0]
    n_meta = meta_ref.shape[0]
    s = s_ref[...]
    for c in range(D_MODEL // COL_CHUNK):
        cols = slice(c * COL_CHUNK, (c + 1) * COL_CHUNK)
        m_ref[:, cols] = (p_ref[:, cols] + gsb_ref[:, cols] * _dot(s, wbs_ref[:, cols])).astype(BF16)
    m = m_ref[...]
    for c in range(D_MODEL // COL_CHUNK):
        cols = slice(c * COL_CHUNK, (c + 1) * COL_CHUNK)
        merged = _dot(m, wo_ref[:, cols])
        if frames_only:
            first = pl.program_id(1) == 0
            head = jnp.where(first, meta_ref[:, cols], x_ref[0:n_meta, cols])
            rest = jnp.where(first, x_ref[0:tm - n_meta, cols], x_ref[n_meta:tm, cols])
            x1_ref[0:n_meta, cols] = head + merged[0:n_meta]
            x1_ref[n_meta:tm, cols] = rest + merged[n_meta:tm]
        else:
            x1_ref[:, cols] = x_ref[:, cols] + merged
    h2_ref[...] = _rms_norm_rows(x1_ref[...], g2_ref[...]).astype(BF16)
    h2 = h2_ref[...]
    for c in range(D_FF // FF_CHUNK):
        cols = slice(c * FF_CHUNK, (c + 1) * FF_CHUNK)
        gt = _dot(h2, wfi_ref[:, cols])
        up = _dot(h2, wfi_ref[:, D_FF + c * FF_CHUNK:D_FF + (c + 1) * FF_CHUNK])
        act_ref[:, cols] = (gt * _sigmoid(gt) * up).astype(BF16)
    act = act_ref[...]
    for c in range(D_MODEL // COL_CHUNK):
        cols = slice(c * COL_CHUNK, (c + 1) * COL_CHUNK)
        y = x1_ref[:, cols] + _dot(act, wfo_ref[:, cols])
        if final_norm:
            x1_ref[:, cols] = y
        else:
            o_ref[:, cols] = y
    if final_norm:
        o_ref[...] = _rms_norm_rows(x1_ref[...], gf_ref[...])


def _output_stage(x, meta, s, p, gsb, wbs, wo, g2, wfi, wfo, gf, *, layer, tm, row0, final_norm):
    b, lp, _ = s.shape
    frames_only = x.shape[1] != lp
    assert not (frames_only and row0), "frames-only residual input needs whole-sequence tiles"
    n_tiles = (lp - row0) // tm
    full = functools.partial(_layer_spec, layer, pipeline_mode=pl.Buffered(1))
    if row0 == 0:
        rows = lambda width: pl.BlockSpec((None, tm, width), lambda bi, j: (bi, j, 0))
    else:
        rows = lambda width: pl.BlockSpec((pl.Squeezed(), pl.Element(tm), pl.Element(width)),
                                          lambda bi, j: (bi, pl.multiple_of(row0 + j * tm, BF16_ROWS), 0))
    x_spec = _frames_window(tm, meta.shape[0], lambda bi, j: (bi, j)) if frames_only else rows(D_MODEL)
    return pl.pallas_call(
        functools.partial(_output_stage_kernel, final_norm, frames_only),
        grid=(b, n_tiles),
        in_specs=[x_spec, pl.BlockSpec(meta.shape, lambda bi, j: (0, 0)),
                  rows(SB_WIDTH), rows(D_MODEL), rows(D_MODEL),
                  full((SB_WIDTH, D_MODEL)), full((D_MODEL, D_MODEL)), full((1, D_MODEL)),
                  full((D_MODEL, 2 * D_FF)), full((D_FF, D_MODEL)),
                  pl.BlockSpec((1, D_MODEL), lambda bi, j: (0, 0))],
        out_specs=pl.BlockSpec((None, tm, D_MODEL), lambda bi, j: (bi, j, 0)),
        out_shape=jax.ShapeDtypeStruct((b, n_tiles * tm, D_MODEL), F32),
        scratch_shapes=[pltpu.VMEM((tm, D_MODEL), BF16),
                        pltpu.VMEM((tm, D_MODEL), F32),
                        pltpu.VMEM((tm, D_MODEL), BF16),
                        pltpu.VMEM((tm, D_FF), BF16)],
        compiler_params=pltpu.CompilerParams(dimension_semantics=("arbitrary", "arbitrary"),
                                             vmem_limit_bytes=VMEM_LIMIT_BYTES),
        name="output_stage",
    )(x, meta, s, p, gsb, wbs, wo, g2, wfi, wfo, gf)


def kernel(x, meta_tokens, norm1_g, w_in, b_gate, pool_mix, pool_scale, w_branch_pool, w_branch_sb,
           w_out, norm2_g, w_ffn_in, w_ffn_out, final_norm_g):
    b, seq, d = x.shape
    depth = w_in.shape[0]
    lp = N_META + seq
    tiles_per_seq = 3
    tm = lp // tiles_per_seq
    assert tm * tiles_per_seq == lp and tm % BF16_ROWS == 0 and tm >= HALO, (lp, tm)
    assert seq % FINAL_TILE == 0 and N_META % BF16_ROWS == 0, (seq, N_META)

    assert depth >= 2 and meta_tokens.shape == (N_META, d), (depth, meta_tokens.shape)
    meta = meta_tokens.astype(x.dtype)

    w_in, pool_mix, w_branch_pool, w_branch_sb, w_out, w_ffn_in, w_ffn_out = (
        w.astype(BF16) for w in (w_in, pool_mix, w_branch_pool, w_branch_sb, w_out, w_ffn_in, w_ffn_out))
    norm1_g, b_gate, pool_scale, norm2_g = (
        a[:, None, :] for a in (norm1_g, b_gate, pool_scale, norm2_g))

    per_seq = lambda a: a.reshape(b, lp, a.shape[-1])
    h = x
    for layer in range(depth):
        last = layer == depth - 1
        q, k, v, p, gsb = _input_stage(
            h if layer == 0 else h.reshape(b * lp, d), meta, norm1_g, w_in, b_gate, pool_mix,
            pool_scale, w_branch_pool, layer=layer, tm=tm, tiles_per_seq=tiles_per_seq)
        s = _attention(per_seq(q), per_seq(k), per_seq(v))
        h = _output_stage(
            h, meta, s, per_seq(p), per_seq(gsb), w_branch_sb, w_out, norm2_g, w_ffn_in,
            w_ffn_out, final_norm_g[None], layer=layer, final_norm=last,
            tm=FINAL_TILE if last else tm, row0=N_META if last else 0)

    return h
```

```python
import functools

import numpy as np

import jax
import jax.numpy as jnp
from jax import lax
from jax.experimental import pallas as pl
from jax.experimental.pallas import tpu as pltpu

D_MODEL = 1024
N_META = 16
SB_HEAD_DIM = 64
SB_WIDTH = 512
POOL_WINDOWS = (2, 4, 8, 16)
POOL_WIDTH = 512
POOL_GROUP_DIM = 128
N_IN = 4096
D_FF = 2816
RMS_EPS = 1e-6

LANES = 128
BF16_ROWS = 16
Q_BLOCK = 128
K_BLOCK = 256
ATTN_SLOTS = 11
ATTN_PAIRS = 2
HALO = 16
FINAL_TILE = 512
FF_CHUNK = 256
COL_CHUNK = 512
VMEM_LIMIT_BYTES = 56 * 1024 * 1024
MASKED_LOGIT = -1e30

BF16 = jnp.bfloat16
F32 = jnp.float32


def _dot(a, b):
    return jnp.dot(a, b, preferred_element_type=F32)


def _rms_norm_rows(x, g):
    ms = jnp.mean(x * x, axis=-1, keepdims=True)
    return x * lax.rsqrt(ms + RMS_EPS) * g


def _sigmoid(x):
    return 1.0 / (1.0 + jnp.exp(-x))


def _input_stage_kernel(tiles_per_seq, frames_only, x_ref, meta_ref, g_ref, w_ref, bg_ref, mix_ref,
                        ps_ref, wbp_ref, q_ref, k_ref, v_ref, p_ref, gsb_ref,
                        h_ref, ubuf_ref, a_ref):
    tm = x_ref.shape[0]
    tile_in_seq = pl.program_id(0) % tiles_per_seq
    n_meta = meta_ref.shape[0]

    @pl.when(tile_in_seq == 0)
    def _():
        ubuf_ref[0:HALO, :] = jnp.zeros((HALO, POOL_WIDTH), F32)

    @pl.when(tile_in_seq != 0)
    def _():
        ubuf_ref[0:HALO, :] = ubuf_ref[tm:tm + HALO, :]

    if frames_only:
        first = tile_in_seq == 0
        head = jnp.where(first, meta_ref[...], x_ref[0:n_meta, :])
        rest = jnp.where(first, x_ref[0:tm - n_meta, :], x_ref[n_meta:tm, :])
        h_ref[0:n_meta, :] = _rms_norm_rows(head, g_ref[...]).astype(BF16)
        h_ref[n_meta:tm, :] = _rms_norm_rows(rest, g_ref[...]).astype(BF16)
    else:
        h_ref[...] = _rms_norm_rows(x_ref[...], g_ref[...]).astype(BF16)

    h = h_ref[...]
    ubuf_ref[HALO:HALO + tm, :] = _dot(h, w_ref[:, 3 * SB_WIDTH:3 * SB_WIDTH + POOL_WIDTH])

    def gate(branch_ref, c):
        first = D_MODEL * (branch_ref is gsb_ref) + c * COL_CHUNK
        w0 = 3 * SB_WIDTH + POOL_WIDTH + first
        branch_ref[:, c * COL_CHUNK:(c + 1) * COL_CHUNK] = _sigmoid(
            _dot(h, w_ref[:, w0:w0 + COL_CHUNK]) + bg_ref[:, first:first + COL_CHUNK])

    gates = [(ref, c) for c in range(D_MODEL // COL_CHUNK) for ref in (p_ref, gsb_ref)]
    pos = tile_in_seq * tm + lax.broadcasted_iota(jnp.int32, (tm, 1), 0)
    for g, window in enumerate(POOL_WINDOWS):
        gate(*gates[g])
        cols = slice(g * POOL_GROUP_DIM, (g + 1) * POOL_GROUP_DIM)
        ug = ubuf_ref[HALO:HALO + tm, cols]
        total = ug
        for d in range(1, window):
            total = total + ubuf_ref[HALO - d:HALO - d + tm, cols]
        cnt = jnp.minimum(pos + 1, window).astype(F32)
        diff = (total / cnt - ug).astype(BF16)
        a_ref[:, cols] = (_dot(diff, mix_ref[g]) * ps_ref[:, cols]).astype(BF16)
    for extra in gates[len(POOL_WINDOWS):]:
        gate(*extra)

    q_ref[...] = (_dot(h, w_ref[:, 0:SB_WIDTH]) * (SB_HEAD_DIM ** -0.5)).astype(BF16)
    k_ref[...] = _dot(h, w_ref[:, SB_WIDTH:2 * SB_WIDTH]).astype(BF16)
    v_ref[...] = _dot(h, w_ref[:, 2 * SB_WIDTH:3 * SB_WIDTH]).astype(BF16)

    a = a_ref[...]
    for c in range(D_MODEL // COL_CHUNK):
        cols = slice(c * COL_CHUNK, (c + 1) * COL_CHUNK)
        p_ref[:, cols] = p_ref[:, cols] * _dot(a, wbp_ref[:, cols])


def _layer_spec(layer, shape, **kwargs):
    return pl.BlockSpec((None,) + shape, lambda *_: (layer,) + (0,) * len(shape), **kwargs)


def _frames_window(tm, n_meta, batch_and_tile):
    def index_map(*grid_idx):
        bi, j = batch_and_tile(*grid_idx)
        return bi, pl.multiple_of(jnp.maximum(j * tm - n_meta, 0), BF16_ROWS), 0
    return pl.BlockSpec((pl.Squeezed(), pl.Element(tm), pl.Element(D_MODEL)), index_map)


def _input_stage(x, meta, g, w, bg, mix, ps, wbp, *, layer, tm, tiles_per_seq):
    frames_only = x.ndim == 3
    t = x.shape[0] * tiles_per_seq * tm if frames_only else x.shape[0]
    full = functools.partial(_layer_spec, layer)
    rows = lambda width: pl.BlockSpec((tm, width), lambda i: (i, 0))
    x_spec = (_frames_window(tm, meta.shape[0], lambda i: (i // tiles_per_seq, i % tiles_per_seq))
              if frames_only else rows(D_MODEL))
    return pl.pallas_call(
        functools.partial(_input_stage_kernel, tiles_per_seq, frames_only),
        grid=(t // tm,),
        in_specs=[x_spec, pl.BlockSpec(meta.shape, lambda i: (0, 0)),
                  full((1, D_MODEL)), full((D_MODEL, N_IN)), full((1, 2 * D_MODEL)),
                  full((len(POOL_WINDOWS), POOL_GROUP_DIM, POOL_GROUP_DIM)), full((1, POOL_WIDTH)),
                  full((POOL_WIDTH, D_MODEL))],
        out_specs=[rows(SB_WIDTH), rows(SB_WIDTH), rows(SB_WIDTH), rows(D_MODEL), rows(D_MODEL)],
        out_shape=[jax.ShapeDtypeStruct((t, SB_WIDTH), BF16)] * 3
                  + [jax.ShapeDtypeStruct((t, D_MODEL), F32)] * 2,
        scratch_shapes=[pltpu.VMEM((tm, D_MODEL), BF16),
                        pltpu.VMEM((tm + HALO, POOL_WIDTH), F32),
                        pltpu.VMEM((tm, POOL_WIDTH), BF16)],
        compiler_params=pltpu.CompilerParams(dimension_semantics=("arbitrary",),
                                             vmem_limit_bytes=VMEM_LIMIT_BYTES),
        name="input_stage",
    )(x, meta, g, w, bg, mix, ps, wbp)


def _attention_schedule(lp):
    kinds = {None: 0}
    blocks = []
    for i in range(lp // Q_BLOCK):
        n = i * Q_BLOCK // K_BLOCK + 1
        for j in range(n):
            if j < n - 1:
                blocks.append((i, j * K_BLOCK // Q_BLOCK, 0))
                continue
            start = min(j * K_BLOCK, lp - K_BLOCK)
            key = (j * K_BLOCK - start, i * Q_BLOCK - start)
            blocks.append((i, start // Q_BLOCK, kinds.setdefault(key, len(kinds))))
    c = np.arange(K_BLOCK)[None, :]
    r = np.arange(Q_BLOCK)[:, None]
    mask = np.zeros((len(kinds), Q_BLOCK, K_BLOCK), np.float32)
    for key, kind in kinds.items():
        if key is not None:
            lo, off = key
            mask[kind] = np.where((c >= lo) & (c < r + off), 0.0, MASKED_LOGIT)
    return np.asarray(blocks, np.int32), mask


def _attention_kernel(blocks, q_ref, k_ref, v_ref, tri_ref, mask_ref, eye_ref, o_ref,
                      qs_ref, kt_ref, vc_ref, acc_ref, z_ref, sp_ref, incl_ref, w_ref):
    n_blocks = len(blocks)
    n_q = qs_ref.shape[0]
    seq_len = q_ref.shape[1]
    lane = lax.broadcasted_iota(jnp.int32, (Q_BLOCK, LANES), 1)

    n_chunks = -(-seq_len // Q_BLOCK)

    def chunk(ref, idx):
        p, i = divmod(idx, n_chunks)
        n_rows = min(Q_BLOCK, seq_len - i * Q_BLOCK)
        rows = ref[0, i * Q_BLOCK:i * Q_BLOCK + n_rows, p * LANES:(p + 1) * LANES]
        if n_rows < Q_BLOCK:
            rows = jnp.concatenate([rows, jnp.zeros((Q_BLOCK - n_rows, LANES), rows.dtype)], axis=0)
        return rows

    for i in range(n_q):
        q2 = chunk(q_ref, i)
        zero = jnp.zeros_like(q2)
        qs_ref[i, 0:Q_BLOCK, :] = jnp.where(lane < SB_HEAD_DIM, q2, zero)
        qs_ref[i, Q_BLOCK:2 * Q_BLOCK, :] = jnp.where(lane >= SB_HEAD_DIM, q2, zero)
        kt_ref[i] = chunk(k_ref, i).T
        vc_ref[i] = chunk(v_ref, i)
    acc_ref[...] = jnp.zeros_like(acc_ref)

    def block(t):
        return blocks[t]

    def raw_scores(t, slot):
        i, c, kind = block(t)
        lhs = jnp.concatenate([qs_ref[i], eye_ref[...]], axis=1)
        keys_t = jnp.concatenate([kt_ref[c], kt_ref[c + 1]], axis=1)
        z_ref[slot] = _dot(lhs, jnp.concatenate([keys_t, mask_ref[kind]], axis=0))

    def softplus(t, slot):
        z = z_ref[slot].astype(BF16)
        sp_ref[slot] = jnp.maximum(z, 0.0) + jnp.log(1.0 + jnp.exp(-jnp.abs(z)))

    def suffix_sums(t, slot):
        incl_ref[slot] = _dot(sp_ref[slot], tri_ref[...])

    def weights(t, slot):
        w_ref[slot] = jnp.exp(z_ref[slot] + incl_ref[slot]).astype(BF16)

    def accumulate(t, slot):
        i, c, _ = block(t)
        decay = jnp.exp(incl_ref[slot, :, 0:1])
        values = jnp.concatenate([vc_ref[c], vc_ref[c + 1]], axis=0)
        acc_ref[i] = acc_ref[i] * decay + _dot(w_ref[slot], values)

    stages = (raw_scores, softplus, suffix_sums, weights, accumulate)
    depth = len(stages) - 1

    for t in range(n_blocks + depth):
        for s in range(depth, -1, -1):
            if 0 <= t - s < n_blocks:
                stages[s](t - s, (t - s) % ATTN_SLOTS)

    for idx in range(n_q):
        p, i = divmod(idx, n_chunks)
        n_rows = min(Q_BLOCK, seq_len - i * Q_BLOCK)
        acc = acc_ref[idx]
        both = jnp.where(lane < SB_HEAD_DIM, acc[:Q_BLOCK], acc[Q_BLOCK:]).astype(o_ref.dtype)
        o_ref[0, i * Q_BLOCK:i * Q_BLOCK + n_rows, p * LANES:(p + 1) * LANES] = both[:n_rows]


def _attention(q, k, v):
    b, lp, _ = q.shape
    n_pairs = SB_WIDTH // LANES
    assert lp % BF16_ROWS == 0 and lp >= K_BLOCK, lp
    n_q = -(-lp // Q_BLOCK)
    blocks, mask = _attention_schedule(n_q * Q_BLOCK)
    blocks = tuple((int(i) + p * n_q, int(c) + p * n_q, int(kind))
                   for p in range(ATTN_PAIRS) for i, c, kind in blocks)
    n_q *= ATTN_PAIRS
    j = np.arange(K_BLOCK)[:, None]
    s = np.arange(K_BLOCK)[None, :]
    tri = jnp.asarray(np.where(j >= s, -1.0, 0.0), BF16)
    eye = jnp.asarray(np.tile(np.eye(Q_BLOCK), (2, 1)), BF16)
    mask = jnp.asarray(mask, BF16)
    seq = lambda: pl.BlockSpec((1, lp, ATTN_PAIRS * LANES), lambda bi, hp: (bi, 0, hp))
    return pl.pallas_call(
        functools.partial(_attention_kernel, blocks),
        grid=(b, n_pairs // ATTN_PAIRS),
        in_specs=[seq(), seq(), seq(),
                  pl.BlockSpec(tri.shape, lambda bi, hp: (0, 0)),
                  pl.BlockSpec(mask.shape, lambda bi, hp: (0, 0, 0)),
                  pl.BlockSpec(eye.shape, lambda bi, hp: (0, 0))],
        out_specs=seq(),
        scratch_shapes=[pltpu.VMEM((n_q, 2 * Q_BLOCK, LANES), BF16),
                        pltpu.VMEM((n_q, LANES, Q_BLOCK), BF16),
                        pltpu.VMEM((n_q, Q_BLOCK, LANES), BF16),
                        pltpu.VMEM((n_q, 2 * Q_BLOCK, LANES), F32),
                        pltpu.VMEM((ATTN_SLOTS, 2 * Q_BLOCK, K_BLOCK), F32),
                        pltpu.VMEM((ATTN_SLOTS, 2 * Q_BLOCK, K_BLOCK), BF16),
                        pltpu.VMEM((ATTN_SLOTS, 2 * Q_BLOCK, K_BLOCK), F32),
                        pltpu.VMEM((ATTN_SLOTS, 2 * Q_BLOCK, K_BLOCK), BF16)],
        out_shape=jax.ShapeDtypeStruct((b, lp, SB_WIDTH), BF16),
        compiler_params=pltpu.CompilerParams(dimension_semantics=("arbitrary", "arbitrary")),
        name="stick_breaking_attention",
    )(q, k, v, tri, mask, eye)


def _output_stage_kernel(final_norm, frames_only, x_ref, meta_ref, s_ref, p_ref, gsb_ref, wbs_ref,
                         wo_ref, g2_ref, wfi_ref, wfo_ref, gf_ref, o_ref,
                         m_ref, x1_ref, h2_ref, act_ref):
    tm = x1_ref.shape[0]
    n_meta = meta_ref.shape[0]
    s = s_ref[...]
    for c in range(D_MODEL // COL_CHUNK):
        cols = slice(c * COL_CHUNK, (c + 1) * COL_CHUNK)
        m_ref[:, cols] = (p_ref[:, cols] + gsb_ref[:, cols] * _dot(s, wbs_ref[:, cols])).astype(BF16)
    m = m_ref[...]
    for c in range(D_MODEL // COL_CHUNK):
        cols = slice(c * COL_CHUNK, (c + 1) * COL_CHUNK)
        merged = _dot(m, wo_ref[:, cols])
        if frames_only:
            first = pl.program_id(1) == 0
            head = jnp.where(first, meta_ref[:, cols], x_ref[0:n_meta, cols])
            rest = jnp.where(first, x_ref[0:tm - n_meta, cols], x_ref[n_meta:tm, cols])
            x1_ref[0:n_meta, cols] = head + merged[0:n_meta]
            x1_ref[n_meta:tm, cols] = rest + merged[n_meta:tm]
        else:
            x1_ref[:, cols] = x_ref[:, cols] + merged
    h2_ref[...] = _rms_norm_rows(x1_ref[...], g2_ref[...]).astype(BF16)
    h2 = h2_ref[...]
    for c in range(D_FF // FF_CHUNK):
        cols = slice(c * FF_CHUNK, (c + 1) * FF_CHUNK)
        gt = _dot(h2, wfi_ref[:, cols])
        up = _dot(h2, wfi_ref[:, D_FF + c * FF_CHUNK:D_FF + (c + 1) * FF_CHUNK])
        act_ref[:, cols] = (gt * _sigmoid(gt) * up).astype(BF16)
    act = act_ref[...]
    for c in range(D_MODEL // COL_CHUNK):
        cols = slice(c * COL_CHUNK, (c + 1) * COL_CHUNK)
        y = x1_ref[:, cols] + _dot(act, wfo_ref[:, cols])
        if final_norm:
            x1_ref[:, cols] = y
        else:
            o_ref[:, cols] = y
    if final_norm:
        o_ref[...] = _rms_norm_rows(x1_ref[...], gf_ref[...])


def _output_stage(x, meta, s, p, gsb, wbs, wo, g2, wfi, wfo, gf, *, layer, tm, row0, final_norm):
    b, lp, _ = s.shape
    frames_only = x.shape[1] != lp
    assert not (frames_only and row0), "frames-only residual input needs whole-sequence tiles"
    n_tiles = (lp - row0) // tm
    full = functools.partial(_layer_spec, layer, pipeline_mode=pl.Buffered(1))
    if row0 == 0:
        rows = lambda width: pl.BlockSpec((None, tm, width), lambda bi, j: (bi, j, 0))
    else:
        rows = lambda width: pl.BlockSpec((pl.Squeezed(), pl.Element(tm), pl.Element(width)),
                                          lambda bi, j: (bi, pl.multiple_of(row0 + j * tm, BF16_ROWS), 0))
    x_spec = _frames_window(tm, meta.shape[0], lambda bi, j: (bi, j)) if frames_only else rows(D_MODEL)
    return pl.pallas_call(
        functools.partial(_output_stage_kernel, final_norm, frames_only),
        grid=(b, n_tiles),
        in_specs=[x_spec, pl.BlockSpec(meta.shape, lambda bi, j: (0, 0)),
                  rows(SB_WIDTH), rows(D_MODEL), rows(D_MODEL),
                  full((SB_WIDTH, D_MODEL)), full((D_MODEL, D_MODEL)), full((1, D_MODEL)),
                  full((D_MODEL, 2 * D_FF)), full((D_FF, D_MODEL)),
                  pl.BlockSpec((1, D_MODEL), lambda bi, j: (0, 0))],
        out_specs=pl.BlockSpec((None, tm, D_MODEL), lambda bi, j: (bi, j, 0)),
        out_shape=jax.ShapeDtypeStruct((b, n_tiles * tm, D_MODEL), F32),
        scratch_shapes=[pltpu.VMEM((tm, D_MODEL), BF16),
                        pltpu.VMEM((tm, D_MODEL), F32),
                        pltpu.VMEM((tm, D_MODEL), BF16),
                        pltpu.VMEM((tm, D_FF), BF16)],
        compiler_params=pltpu.CompilerParams(dimension_semantics=("arbitrary", "arbitrary"),
                                             vmem_limit_bytes=VMEM_LIMIT_BYTES),
        name="output_stage",
    )(x, meta, s, p, gsb, wbs, wo, g2, wfi, wfo, gf)


def kernel(x, meta_tokens, norm1_g, w_in, b_gate, pool_mix, pool_scale, w_branch_pool, w_branch_sb,
           w_out, norm2_g, w_ffn_in, w_ffn_out, final_norm_g):
    b, seq, d = x.shape
    depth = w_in.shape[0]
    lp = N_META + seq
    tiles_per_seq = 3
    tm = lp // tiles_per_seq
    assert tm * tiles_per_seq == lp and tm % BF16_ROWS == 0 and tm >= HALO, (lp, tm)
    assert seq % FINAL_TILE == 0 and N_META % BF16_ROWS == 0, (seq, N_META)

    assert depth >= 2 and meta_tokens.shape == (N_META, d), (depth, meta_tokens.shape)
    meta = meta_tokens.astype(x.dtype)

    w_in, pool_mix, w_branch_pool, w_branch_sb, w_out, w_ffn_in, w_ffn_out = (
        w.astype(BF16) for w in (w_in, pool_mix, w_branch_pool, w_branch_sb, w_out, w_ffn_in, w_ffn_out))
    norm1_g, b_gate, pool_scale, norm2_g = (
        a[:, None, :] for a in (norm1_g, b_gate, pool_scale, norm2_g))

    per_seq = lambda a: a.reshape(b, lp, a.shape[-1])
    h = x
    for layer in range(depth):
        last = layer == depth - 1
        q, k, v, p, gsb = _input_stage(
            h if layer == 0 else h.reshape(b * lp, d), meta, norm1_g, w_in, b_gate, pool_mix,
            pool_scale, w_branch_pool, layer=layer, tm=tm, tiles_per_seq=tiles_per_seq)
        s = _attention(per_seq(q), per_seq(k), per_seq(v))
        h = _output_stage(
            h, meta, s, per_seq(p), per_seq(gsb), w_branch_sb, w_out, norm2_g, w_ffn_in,
            w_ffn_out, final_norm_g[None], layer=layer, final_norm=last,
            tm=FINAL_TILE if last else tm, row0=N_META if last else 0)

    return h
```
